```python
import jax, jax.numpy as jnp
from jax import lax
import numpy as np

D_MODEL = 4096
BATCH = 4
SEQ = 4096
DEPTH = 4
DEC_BATCH = 8
DEC_SEQ = 16
PAST_LEN = 4096

CHUNK = 64
N_META = 16
SSD_EXPAND = 2
D_INNER = SSD_EXPAND * D_MODEL
HEAD_DIM = 64
N_HEADS = D_INNER // HEAD_DIM
N_GROUPS = 8
HEADS_PER_GROUP = N_HEADS // N_GROUPS
D_STATE = 128
SSD_CONV = 4
CONV_DIM = D_INNER + 2 * N_GROUPS * D_STATE
SSD_PROJ = 2 * D_INNER + 2 * N_GROUPS * D_STATE + N_HEADS
SC_WIDTH = D_MODEL
SC_CONV = 3
SC_PROJ = 4 * SC_WIDTH
N_SSD = (DEPTH + 1) // 2
N_SC = DEPTH // 2
EPS = 1e-5
DT_MIN = 1e-3
DT_MAX = 1e-1

kernel_name = 'hybrid_ssd_shortconv_stream_step'


def rmsnorm(x, w):
    xf = x.astype(jnp.float32)
    y = xf * lax.rsqrt(jnp.mean(xf * xf, axis=-1, keepdims=True) + EPS)
    return (y * w.astype(jnp.float32)).astype(x.dtype)


def causal_dwconv(u, buf, w):
    K = w.shape[0]
    L = u.shape[1]
    full = jnp.concatenate([buf.astype(u.dtype), u], axis=1)
    out = full[:, 0:L] * w[0]
    for k in range(1, K):
        out = out + full[:, k:k + L] * w[k]
    return out, full[:, L:]


def ssd_scan(xdt, a, bm, cm, h0):
    b, L = a.shape[:2]
    nc = L // CHUNK

    def to_chunks(t):
        return jnp.moveaxis(t.reshape((b, nc, CHUNK) + t.shape[2:]), 1, 0)

    causal = jnp.tril(jnp.ones((CHUNK, CHUNK), dtype=bool))[:, :, None, None]

    def step(h, inp):
        xc, ac, bc, cc = inp
        a_cs = jnp.cumsum(ac, axis=1)
        seg = a_cs[:, :, None] - a_cs[:, None, :]
        decay = jnp.exp(jnp.where(causal, seg, -jnp.inf))
        cb = jnp.einsum('btgn,bsgn->btsg', cc, bc)
        y = jnp.einsum('btsg,btsgh,bsghp->btghp', cb, decay, xc)
        y = y + jnp.einsum('btgn,bghpn->btghp', cc, h) * jnp.exp(a_cs)[..., None]
        to_end = jnp.exp(a_cs[:, -1:] - a_cs)
        h = h * jnp.exp(a_cs[:, -1])[..., None, None] + jnp.einsum('bsgn,bsgh,bsghp->bghpn', bc, to_end, xc)
        return h, y

    h, ys = lax.scan(step, h0, (to_chunks(xdt), to_chunks(a), to_chunks(bm), to_chunks(cm)))
    y = jnp.moveaxis(ys, 0, 1).reshape(xdt.shape)
    return y, h


def ssd_mixer(u, conv_buf, h0, w_in, conv_w, conv_b, dt_bias, a_log, d_skip, norm_w, w_out):
    b, L, _ = u.shape
    f32 = jnp.float32
    proj = u @ w_in
    z, xbc, dt_raw = jnp.split(proj, [D_INNER, D_INNER + CONV_DIM], axis=-1)
    xbc, new_buf = causal_dwconv(xbc, conv_buf, conv_w)
    xbc = jax.nn.silu(xbc + conv_b)
    xs, bm, cm = jnp.split(xbc, [D_INNER, D_INNER + N_GROUPS * D_STATE], axis=-1)
    xs = xs.astype(f32).reshape(b, L, N_GROUPS, HEADS_PER_GROUP, HEAD_DIM)
    bm = bm.astype(f32).reshape(b, L, N_GROUPS, D_STATE)
    cm = cm.astype(f32).reshape(b, L, N_GROUPS, D_STATE)
    dt = jax.nn.softplus(dt_raw.astype(f32) + dt_bias.astype(f32)).reshape(b, L, N_GROUPS, HEADS_PER_GROUP)
    A = -jnp.exp(a_log.astype(f32)).reshape(N_GROUPS, HEADS_PER_GROUP)
    pad = (-L) % CHUNK
    def padf(t):
        return jnp.pad(t, [(0, 0), (pad, 0)] + [(0, 0)] * (t.ndim - 2))
    h0g = h0.astype(f32).reshape(b, N_GROUPS, HEADS_PER_GROUP, HEAD_DIM, D_STATE)
    y, h = ssd_scan(padf(xs * dt[..., None]), padf(dt * A), padf(bm), padf(cm), h0g)
    y = y[:, pad:] + xs * d_skip.astype(f32).reshape(N_GROUPS, HEADS_PER_GROUP, 1)
    y = y.reshape(b, L, N_GROUPS, D_INNER // N_GROUPS) * jax.nn.silu(z.astype(f32)).reshape(b, L, N_GROUPS, D_INNER // N_GROUPS)
    y = y * lax.rsqrt(jnp.mean(y * y, axis=-1, keepdims=True) + EPS)
    y = (y.reshape(b, L, D_INNER) * norm_w.astype(f32)).astype(u.dtype)
    h = h.reshape(b, N_HEADS, HEAD_DIM, D_STATE).astype(u.dtype)
    return y @ w_out, new_buf, h


def shortconv_mixer(u, conv_buf, w_in, conv_w, w_out):
    g, bgate, cgate, v = jnp.split(u @ w_in, 4, axis=-1)
    cv, new_buf = causal_dwconv(cgate * v, conv_buf, conv_w)
    y = jax.nn.silu(g) * bgate * cv
    return y @ w_out, new_buf


def trunk(h, ssm_states, ssd_bufs, sc_bufs, ln_w, ssd_w_in, ssd_conv_w, ssd_conv_b, ssd_dt_bias,
          ssd_a_log, ssd_d_skip, ssd_norm_w, ssd_w_out, sc_w_in, sc_conv_w, sc_w_out, final_norm_w):
    new_ssm, new_ssd_buf, new_sc_buf = [], [], []
    for i in range(DEPTH):
        j = i // 2
        u = rmsnorm(h, ln_w[i])
        if i % 2 == 0:
            out, buf, st = ssd_mixer(u, ssd_bufs[j], ssm_states[j], ssd_w_in[j], ssd_conv_w[j], ssd_conv_b[j],
                                     ssd_dt_bias[j], ssd_a_log[j], ssd_d_skip[j], ssd_norm_w[j], ssd_w_out[j])
            new_ssm.append(st)
            new_ssd_buf.append(buf)
        else:
            out, buf = shortconv_mixer(u, sc_bufs[j], sc_w_in[j], sc_conv_w[j], sc_w_out[j])
            new_sc_buf.append(buf)
        h = h + out
    y = rmsnorm(h, final_norm_w)
    return y, jnp.stack(new_ssm), jnp.stack(new_ssd_buf), jnp.stack(new_sc_buf)


def setup_inputs(seed: int = 0) -> dict:
    key = jax.random.key(seed)
    ks = jax.random.split(key, 24)
    f32 = jnp.float32
    nrm = lambda k, s, sc: jax.random.normal(k, s, f32) * sc
    dt0 = jnp.exp(jax.random.uniform(ks[10], (N_SSD, N_HEADS), f32, np.log(DT_MIN), np.log(DT_MAX)))
    return {
        'x_prompt': nrm(ks[0], (BATCH, SEQ, D_MODEL), 1.0),
        'x_sample': nrm(ks[1], (DEC_BATCH, DEC_SEQ, D_MODEL), 1.0),
        'state_ssm': nrm(ks[2], (N_SSD, DEC_BATCH, N_HEADS, HEAD_DIM, D_STATE), 0.1),
        'state_ssd_conv': nrm(ks[3], (N_SSD, DEC_BATCH, SSD_CONV - 1, CONV_DIM), 1.0),
        'state_sc_conv': nrm(ks[4], (N_SC, DEC_BATCH, SC_CONV - 1, SC_WIDTH), 1.0),
        'meta_tokens': nrm(ks[5], (N_META, D_MODEL), 1.0),
        'ln_w': 1.0 + nrm(ks[6], (DEPTH, D_MODEL), 0.02),
        'ssd_w_in': nrm(ks[7], (N_SSD, D_MODEL, SSD_PROJ), D_MODEL ** -0.5),
        'ssd_conv_w': nrm(ks[8], (N_SSD, SSD_CONV, CONV_DIM), SSD_CONV ** -0.5),
        'ssd_conv_b': nrm(ks[9], (N_SSD, CONV_DIM), 0.02),
        'ssd_dt_bias': dt0 + jnp.log(-jnp.expm1(-dt0)),
        'ssd_a_log': jnp.log(jax.random.uniform(ks[11], (N_SSD, N_HEADS), f32, 1.0, 16.0)),
        'ssd_d_skip': 1.0 + nrm(ks[12], (N_SSD, N_HEADS), 0.1),
        'ssd_norm_w': 1.0 + nrm(ks[13], (N_SSD, D_INNER), 0.02),
        'ssd_w_out': nrm(ks[14], (N_SSD, D_INNER, D_MODEL), D_INNER ** -0.5),
        'sc_w_in': nrm(ks[15], (N_SC, D_MODEL, SC_PROJ), D_MODEL ** -0.5),
        'sc_conv_w': nrm(ks[16], (N_SC, SC_CONV, SC_WIDTH), SC_CONV ** -0.5),
        'sc_w_out': nrm(ks[17], (N_SC, SC_WIDTH, D_MODEL), SC_WIDTH ** -0.5),
        'final_norm_w': 1.0 + nrm(ks[18], (D_MODEL,), 0.02),
    }


def reference(x_prompt, x_sample, state_ssm, state_ssd_conv, state_sc_conv, meta_tokens, ln_w, ssd_w_in,
              ssd_conv_w, ssd_conv_b, ssd_dt_bias, ssd_a_log, ssd_d_skip, ssd_norm_w, ssd_w_out,
              sc_w_in, sc_conv_w, sc_w_out, final_norm_w):
    weights = (ln_w, ssd_w_in, ssd_conv_w, ssd_conv_b, ssd_dt_bias, ssd_a_log, ssd_d_skip, ssd_norm_w,
               ssd_w_out, sc_w_in, sc_conv_w, sc_w_out, final_norm_w)
    dt = x_prompt.dtype
    b = x_prompt.shape[0]
    meta = jnp.broadcast_to(meta_tokens.astype(dt)[None], (b, N_META, D_MODEL))
    h_p = jnp.concatenate([meta, x_prompt], axis=1)
    ssm0 = jnp.zeros((N_SSD, b, N_HEADS, HEAD_DIM, D_STATE), dt)
    ssdbuf0 = jnp.zeros((N_SSD, b, SSD_CONV - 1, CONV_DIM), dt)
    scbuf0 = jnp.zeros((N_SC, b, SC_CONV - 1, SC_WIDTH), dt)
    y_p, ssm_p, ssdbuf_p, scbuf_p = trunk(h_p, ssm0, ssdbuf0, scbuf0, *weights)
    y_prompt = y_p[:, N_META:]
    y_sample, ssm_s, ssdbuf_s, scbuf_s = trunk(x_sample, state_ssm, state_ssd_conv, state_sc_conv, *weights)
    return (y_prompt, y_sample, ssm_p, ssdbuf_p, scbuf_p, ssm_s, ssdbuf_s, scbuf_s)
```

```python
import functools

import jax
import jax.numpy as jnp
from jax import lax
from jax.experimental import pallas as pl
from jax.experimental.pallas import tpu as pltpu

EPS = 1e-5
HEAD_DIM = 64
D_STATE = 128
N_GROUPS = 8
LANES = 128
SUBLANES = 8
CHUNK = 128
HALO = SUBLANES
VMEM_LIMIT = 56 * 1024 * 1024

f32 = jnp.float32
bf16 = jnp.bfloat16


def _sigmoid(x):
    return 1.0 / (1.0 + jnp.exp(-x))


def _params(*sem):
    return pltpu.CompilerParams(dimension_semantics=sem, vmem_limit_bytes=VMEM_LIMIT)


def _rmsnorm_kernel(x_ref, w_ref, o_ref):
    x = x_ref[...]
    ms = jnp.mean(x * x, axis=-1, keepdims=True)
    o_ref[...] = (x * lax.rsqrt(ms + EPS) * w_ref[...]).astype(o_ref.dtype)


def _rmsnorm(x, w, out_dtype):
    m, d = x.shape
    tm = 256 if m % 256 == 0 else m
    return pl.pallas_call(
        _rmsnorm_kernel,
        grid=(m // tm,),
        in_specs=[pl.BlockSpec((tm, d), lambda i: (i, 0)),
                  pl.BlockSpec((1, d), lambda i: (0, 0))],
        out_specs=pl.BlockSpec((tm, d), lambda i: (i, 0)),
        out_shape=jax.ShapeDtypeStruct((m, d), out_dtype),
        compiler_params=_params("arbitrary"),
        name="rmsnorm",
    )(x, w.reshape(1, d))


def _mm_kernel(x_ref, w_ref, o_ref):
    o_ref[...] = jnp.dot(x_ref[...], w_ref[...], preferred_element_type=f32).astype(o_ref.dtype)


def _mm_res_kernel(x_ref, w_ref, r_ref, o_ref):
    acc = jnp.dot(x_ref[...], w_ref[...], preferred_element_type=f32)
    o_ref[...] = (r_ref[...] + acc).astype(o_ref.dtype)


def _pick(n, candidates):
    for c in candidates:
        if n % c == 0:
            return c
    return n


def _mm_tiles(m, k, n, with_res):
    tm = _pick(m, (1024, 512, 256))
    budget = 44 * 1024 * 1024
    for tn in (1024, 512, 256, 128):
        if n % tn:
            continue
        need = 2 * (tm * k * 2 + k * tn * 2 + tm * tn * 4 * (2 if with_res else 1))
        if need <= budget:
            return tm, tn
    return tm, _pick(n, (128,))


def _matmul(x, w, res=None, out_dtype=f32):
    m, k = x.shape
    n = w.shape[1]
    tm, tn = _mm_tiles(m, k, n, res is not None)
    in_specs = [pl.BlockSpec((tm, k), lambda i, j: (i, 0)),
                pl.BlockSpec((k, tn), lambda i, j: (0, j))]
    args = [x, w]
    kern = _mm_kernel
    if res is not None:
        in_specs.append(pl.BlockSpec((tm, tn), lambda i, j: (i, j)))
        args.append(res)
        kern = _mm_res_kernel
    return pl.pallas_call(
        kern,
        grid=(m // tm, n // tn),
        in_specs=in_specs,
        out_specs=pl.BlockSpec((tm, tn), lambda i, j: (i, j)),
        out_shape=jax.ShapeDtypeStruct((m, n), out_dtype),
        compiler_params=_params("arbitrary", "arbitrary"),
        name="matmul_res" if res is not None else "matmul",
    )(*args)


def _conv_silu(stage, w_ref, b_ref, lo, hi, taps):
    base = HALO - (taps - 1)
    acc = stage[base:base + CHUNK, lo:hi] * w_ref[0:1, lo:hi]
    for k in range(1, taps):
        acc = acc + stage[base + k:base + k + CHUNK, lo:hi] * w_ref[k:k + 1, lo:hi]
    acc = acc + b_ref[:, lo:hi]
    return acc * _sigmoid(acc)


def _ssd_kernel(z_ref, x_ref, bc_ref, dt_ref, cxi_ref, cbci_ref, si_ref,
                cwx_ref, cwbc_ref, cbx_ref, cbbc_ref, dtb_ref, alog_ref, dsk_ref, nw_ref,
                y_ref, cxo_ref, cbco_ref, so_ref,
                xst, bcst, st, *, rows, n_chunks, taps):
    q = CHUNK
    gw = HEAD_DIM * (x_ref.shape[2] // HEAD_DIM // N_GROUPS)
    heads_per_group = gw // HEAD_DIM
    c = pl.program_id(1)

    @pl.when(c == 0)
    def _():
        if rows < q:
            xst[...] = jnp.zeros_like(xst)
            bcst[...] = jnp.zeros_like(bcst)
        xst[0:HALO, :] = cxi_ref[0]
        bcst[0:HALO, :] = cbci_ref[0]
        st[...] = si_ref[0]

    xst[HALO:HALO + rows, :] = x_ref[0]
    bcst[HALO:HALO + rows, :] = bc_ref[0]

    def pad_rows(v):
        if rows == q:
            return v
        return jnp.concatenate([v, jnp.zeros((q - rows, v.shape[1]), v.dtype)], axis=0)

    row_i = lax.broadcasted_iota(jnp.int32, (q, q), 0)
    col_i = lax.broadcasted_iota(jnp.int32, (q, q), 1)
    causal = row_i >= col_i
    v = pad_rows(dt_ref[0]) + dtb_ref[...]
    dt = jnp.maximum(v, 0.0) + jnp.log1p(jnp.exp(-jnp.abs(v)))
    if rows < q:
        dt = jnp.where(lax.broadcasted_iota(jnp.int32, dt.shape, 0) < rows, dt, 0.0)
    a = dt * (-jnp.exp(alog_ref[...]))
    acs = jnp.dot(causal.astype(f32), a, precision=lax.Precision.HIGHEST,
                  preferred_element_type=f32)
    acs_t = acs.T
    dt_t = dt.T
    w_t = dt_t * jnp.exp(acs_t[:, q - 1:q] - acs_t)

    n_state_cols = D_STATE * N_GROUPS
    b_all = _conv_silu(bcst, cwbc_ref, cbbc_ref, 0, n_state_cols, taps)
    c_all = _conv_silu(bcst, cwbc_ref, cbbc_ref, n_state_cols, 2 * n_state_cols, taps)
    lane = lax.broadcasted_iota(jnp.int32, (q, LANES), 1)
    first_head = lane < HEAD_DIM
    neg_inf = jnp.float32(-jnp.inf)

    for g in range(N_GROUPS):
        glo = g * gw
        b_g = b_all[:, g * D_STATE:(g + 1) * D_STATE]
        c_g = c_all[:, g * D_STATE:(g + 1) * D_STATE].astype(bf16)
        cb = lax.dot_general(c_g, b_g.astype(bf16), (((1,), (1,)), ((), ())),
                             preferred_element_type=f32)
        b_t = b_g.T
        s_g = st[:, glo:glo + gw]
        y_off = jnp.dot(c_g, s_g.astype(bf16), preferred_element_type=f32)
        x_g = _conv_silu(xst, cwx_ref, cbx_ref, glo, glo + gw, taps)

        y_parts = []
        for p in range(heads_per_group // 2):
            plo = p * LANES
            x_p = x_g[:, plo:plo + LANES]
            m_parts, bw_parts, e_parts = [], [], []
            for hh in range(2):
                h = g * heads_per_group + 2 * p + hh
                col_b = jnp.broadcast_to(acs[:, h:h + 1], (q, q))
                seg = col_b - acs_t[h:h + 1, :]
                decay = jnp.exp(jnp.where(causal, seg, neg_inf))
                m_parts.append(cb * decay * dt_t[h:h + 1, :])
                bw_parts.append(b_t * w_t[h:h + 1, :])
                e_parts.append(jnp.exp(col_b))
            lhs = jnp.concatenate([jnp.concatenate(m_parts, axis=1),
                                   jnp.concatenate(bw_parts, axis=1)], axis=0).astype(bf16)
            x_bd = jnp.concatenate([jnp.where(first_head, x_p, 0.0),
                                    jnp.where(first_head, 0.0, x_p)], axis=0).astype(bf16)
            res = jnp.dot(lhs, x_bd, preferred_element_type=f32)
            e_sel = jnp.where(first_head, e_parts[0], e_parts[1])
            y_parts.append(res[0:q] + y_off[:, plo:plo + LANES] * e_sel
                           + x_p * dsk_ref[:, glo + plo:glo + plo + LANES])
            st[:, glo + plo:glo + plo + LANES] = (s_g[:, plo:plo + LANES] * e_sel[q - 1:q, :]
                                                  + res[q:q + D_STATE])
        y_g = jnp.concatenate(y_parts, axis=1)
        z_g = pad_rows(z_ref[0, :, glo:glo + gw])
        y_g = y_g * (z_g * _sigmoid(z_g))
        ms = jnp.mean(y_g * y_g, axis=-1, keepdims=True)
        y_n = y_g * lax.rsqrt(ms + EPS) * nw_ref[:, glo:glo + gw]
        y_ref[0, :, glo:glo + gw] = y_n[0:rows].astype(y_ref.dtype)

    @pl.when(c == n_chunks - 1)
    def _():
        cxo_ref[0] = xst[rows:rows + HALO, :]
        cbco_ref[0] = bcst[rows:rows + HALO, :]
        so_ref[0] = st[...]

    if n_chunks > 1:
        xst[0:HALO, :] = xst[q:q + HALO, :]
        bcst[0:HALO, :] = bcst[q:q + HALO, :]


def _ssd_core(proj, dt_raw, conv_x, conv_bc, state, cw_x, cw_bc, cb_x, cb_bc, dt_bias, a_log,
              d_skip_wide, norm_w, rows, shared_state):
    bn, length, _ = proj.shape
    d_inner = cw_x.shape[1]
    n_bc = cw_bc.shape[1]
    n_heads = dt_bias.shape[0]
    taps = cw_x.shape[0]
    n_chunks = length // rows
    sidx = (lambda b, c: (0, 0, 0)) if shared_state else (lambda b, c: (b, 0, 0))
    const = lambda b, c: (0, 0)
    kern = functools.partial(_ssd_kernel, rows=rows, n_chunks=n_chunks, taps=taps)
    return pl.pallas_call(
        kern,
        grid=(bn, n_chunks),
        in_specs=[
            pl.BlockSpec((1, rows, d_inner), lambda b, c: (b, c, 0)),
            pl.BlockSpec((1, rows, d_inner), lambda b, c: (b, c, 1)),
            pl.BlockSpec((1, rows, n_bc), lambda b, c: (b, c, 2 * d_inner // n_bc)),
            pl.BlockSpec((1, rows, n_heads), lambda b, c: (b, c, 0)),
            pl.BlockSpec((1, HALO, d_inner), sidx),
            pl.BlockSpec((1, HALO, n_bc), sidx),
            pl.BlockSpec((1, D_STATE, d_inner), sidx),
            pl.BlockSpec((taps, d_inner), const),
            pl.BlockSpec((taps, n_bc), const),
            pl.BlockSpec((1, d_inner), const),
            pl.BlockSpec((1, n_bc), const),
            pl.BlockSpec((1, n_heads), const),
            pl.BlockSpec((1, n_heads), const),
            pl.BlockSpec((1, d_inner), const),
            pl.BlockSpec((1, d_inner), const),
        ],
        out_specs=[
            pl.BlockSpec((1, rows, d_inner), lambda b, c: (b, c, 0)),
            pl.BlockSpec((1, HALO, d_inner), lambda b, c: (b, 0, 0)),
            pl.BlockSpec((1, HALO, n_bc), lambda b, c: (b, 0, 0)),
            pl.BlockSpec((1, D_STATE, d_inner), lambda b, c: (b, 0, 0)),
        ],
        out_shape=[
            jax.ShapeDtypeStruct((bn, length, d_inner), bf16),
            jax.ShapeDtypeStruct((bn, HALO, d_inner), f32),
            jax.ShapeDtypeStruct((bn, HALO, n_bc), f32),
            jax.ShapeDtypeStruct((bn, D_STATE, d_inner), f32),
        ],
        scratch_shapes=[
            pltpu.VMEM((HALO + CHUNK, d_inner), f32),
            pltpu.VMEM((HALO + CHUNK, n_bc), f32),
            pltpu.VMEM((D_STATE, d_inner), f32),
        ],
        compiler_params=_params("arbitrary", "arbitrary"),
        name="ssd_core",
    )(proj, proj, proj, dt_raw, conv_x, conv_bc, state, cw_x, cw_bc,
      cb_x.reshape(1, -1), cb_bc.reshape(1, -1), dt_bias.reshape(1, -1), a_log.reshape(1, -1),
      d_skip_wide.reshape(1, -1), norm_w.reshape(1, -1))


def _sc_kernel(g_ref, b_ref, c_ref, v_ref, ci_ref, w_ref, y_ref, co_ref, stage, *, rows, n_steps, taps):
    t = pl.program_id(2)

    @pl.when(t == 0)
    def _():
        stage[0:HALO, :] = ci_ref[0]

    stage[HALO:HALO + rows, :] = c_ref[0] * v_ref[0]
    base = HALO - (taps - 1)
    cv = stage[base:base + rows, :] * w_ref[0:1, :]
    for k in range(1, taps):
        cv = cv + stage[base + k:base + k + rows, :] * w_ref[k:k + 1, :]
    gate = g_ref[0]
    y_ref[0] = ((gate * _sigmoid(gate)) * b_ref[0] * cv).astype(y_ref.dtype)

    @pl.when(t == n_steps - 1)
    def _():
        co_ref[0] = stage[rows:rows + HALO, :]

    if n_steps > 1:
        stage[0:HALO, :] = stage[rows:rows + HALO, :]


def _sc_core(proj, conv_in, conv_w, rows, shared_state):
    bn, length, four_w = proj.shape
    width = four_w // 4
    taps = conv_w.shape[0]
    tc = _pick(width, (1024, 512, 256, 128))
    nj = width // tc
    n_steps = length // rows
    sidx = (lambda b, j, t: (0, 0, j)) if shared_state else (lambda b, j, t: (b, 0, j))
    kern = functools.partial(_sc_kernel, rows=rows, n_steps=n_steps, taps=taps)
    col = lambda part: (lambda b, j, t: (b, t, part * nj + j))
    return pl.pallas_call(
        kern,
        grid=(bn, nj, n_steps),
        in_specs=[pl.BlockSpec((1, rows, tc), col(0)),
                  pl.BlockSpec((1, rows, tc), col(1)),
                  pl.BlockSpec((1, rows, tc), col(2)),
                  pl.BlockSpec((1, rows, tc), col(3)),
                  pl.BlockSpec((1, HALO, tc), sidx),
                  pl.BlockSpec((taps, tc), lambda b, j, t: (0, j))],
        out_specs=[pl.BlockSpec((1, rows, tc), lambda b, j, t: (b, t, j)),
                   pl.BlockSpec((1, HALO, tc), lambda b, j, t: (b, 0, j))],
        out_shape=[jax.ShapeDtypeStruct((bn, length, width), bf16),
                   jax.ShapeDtypeStruct((bn, HALO, width), f32)],
        scratch_shapes=[pltpu.VMEM((HALO + rows, tc), f32)],
        compiler_params=_params("arbitrary", "arbitrary", "arbitrary"),
        name="sc_core",
    )(proj, proj, proj, proj, conv_in, conv_w)


def _pad_history(buf):
    return jnp.pad(buf, ((0, 0), (HALO - buf.shape[1], 0), (0, 0)))


def _trunk(h, bn, rows, ssm_states, ssd_hist_x, ssd_hist_bc, sc_hist, wts, shared_state):
    m, d_model = h.shape
    length = m // bn
    depth = wts["ln_w"].shape[0]
    new_ssm, new_hx, new_hbc, new_sc = [], [], [], []
    for i in range(depth):
        j = i // 2
        u = _rmsnorm(h, wts["ln_w"][i], bf16)
        if i % 2 == 0:
            proj = _matmul(u, wts["ssd_w_main"][j])
            dt_raw = _matmul(u, wts["ssd_w_dt"][j])
            d_inner = wts["ssd_w_out"][j].shape[0]
            y, hx, hbc, s = _ssd_core(
                proj.reshape(bn, length, -1), dt_raw.reshape(bn, length, -1),
                ssd_hist_x[j], ssd_hist_bc[j], ssm_states[j],
                wts["ssd_conv_w"][j][:, :d_inner], wts["ssd_conv_w"][j][:, d_inner:],
                wts["ssd_conv_b"][j][:d_inner], wts["ssd_conv_b"][j][d_inner:],
                wts["ssd_dt_bias"][j], wts["ssd_a_log"][j], wts["ssd_d_wide"][j],
                wts["ssd_norm_w"][j], rows, shared_state)
            new_ssm.append(s)
            new_hx.append(hx)
            new_hbc.append(hbc)
            h = _matmul(y.reshape(m, d_inner), wts["ssd_w_out"][j], res=h)
        else:
            proj = _matmul(u, wts["sc_w_in"][j])
            sc_rows = rows if rows < CHUNK else _pick(length, (512, 256, 128))
            y, hist = _sc_core(proj.reshape(bn, length, -1), sc_hist[j], wts["sc_conv_w"][j],
                               sc_rows, shared_state)
            new_sc.append(hist)
            h = _matmul(y.reshape(m, -1), wts["sc_w_out"][j], res=h)
    y = _rmsnorm(h, wts["final_norm_w"], f32)
    return y, new_ssm, new_hx, new_hbc, new_sc


def kernel(x_prompt, x_sample, state_ssm, state_ssd_conv, state_sc_conv, meta_tokens, ln_w, ssd_w_in,
           ssd_conv_w, ssd_conv_b, ssd_dt_bias, ssd_a_log, ssd_d_skip, ssd_norm_w, ssd_w_out,
           sc_w_in, sc_conv_w, sc_w_out, final_norm_w):
    batch, seq, d_model = x_prompt.shape
    dec_batch, dec_seq, _ = x_sample.shape
    n_ssd, _, n_heads, head_dim, d_state = state_ssm.shape
    n_sc = state_sc_conv.shape[0]
    d_inner = n_heads * head_dim
    n_main = ssd_w_in.shape[2] - n_heads
    assert head_dim == HEAD_DIM and d_state == D_STATE
    assert meta_tokens.shape[0] == dec_seq and dec_seq % SUBLANES == 0 and dec_seq <= CHUNK
    assert seq % CHUNK == 0

    wts = {
        "ln_w": ln_w, "final_norm_w": final_norm_w,
        "ssd_w_main": [ssd_w_in[j, :, :n_main].astype(bf16) for j in range(n_ssd)],
        "ssd_w_dt": [ssd_w_in[j, :, n_main:].astype(bf16) for j in range(n_ssd)],
        "ssd_w_out": [ssd_w_out[j].astype(bf16) for j in range(n_ssd)],
        "sc_w_in": [sc_w_in[j].astype(bf16) for j in range(n_sc)],
        "sc_w_out": [sc_w_out[j].astype(bf16) for j in range(n_sc)],
        "ssd_conv_w": ssd_conv_w, "ssd_conv_b": ssd_conv_b, "ssd_dt_bias": ssd_dt_bias,
        "ssd_a_log": ssd_a_log, "ssd_norm_w": ssd_norm_w, "sc_conv_w": sc_conv_w,
        "ssd_d_wide": jnp.repeat(ssd_d_skip, head_dim, axis=1),
    }

    n_small = 1 + dec_batch
    h_small = jnp.concatenate([meta_tokens[None].astype(x_sample.dtype), x_sample], axis=0)
    zero_lead = lambda a: jnp.concatenate([jnp.zeros((1,) + a.shape[1:], a.dtype), a], axis=0)
    ssm_in, hx_in, hbc_in, sc_in = [], [], [], []
    for j in range(n_ssd):
        s = zero_lead(state_ssm[j]).reshape(n_small, d_inner, d_state)
        ssm_in.append(jnp.swapaxes(s, 1, 2))
        hist = _pad_history(zero_lead(state_ssd_conv[j]))
        hx_in.append(hist[:, :, :d_inner])
        hbc_in.append(hist[:, :, d_inner:])
    for j in range(n_sc):
        sc_in.append(_pad_history(zero_lead(state_sc_conv[j])))

    y_small, ssm_s, hx_s, hbc_s, sc_s = _trunk(
        h_small.reshape(n_small * dec_seq, d_model), n_small, dec_seq,
        ssm_in, hx_in, hbc_in, sc_in, wts, shared_state=False)

    y_p, ssm_p, hx_p, hbc_p, sc_p = _trunk(
        x_prompt.reshape(batch * seq, d_model), batch, CHUNK,
        [s[0:1] for s in ssm_s], [a[0:1] for a in hx_s], [a[0:1] for a in hbc_s],
        [a[0:1] for a in sc_s], wts, shared_state=True)

    def ssm_out(parts, lo):
        return jnp.stack([jnp.swapaxes(s[lo:], 1, 2).reshape(-1, n_heads, head_dim, d_state)
                          for s in parts])

    def ssd_hist_out(px, pbc, lo, k):
        return jnp.stack([jnp.concatenate([a[lo:, HALO - k:], b[lo:, HALO - k:]], axis=-1)
                          for a, b in zip(px, pbc)])

    def sc_hist_out(parts, lo, k):
        return jnp.stack([a[lo:, HALO - k:] for a in parts])

    k_ssd = state_ssd_conv.shape[2]
    k_sc = state_sc_conv.shape[2]
    y_prompt = y_p.reshape(batch, seq, d_model)
    y_sample = y_small.reshape(n_small, dec_seq, d_model)[1:]
    return (y_prompt, y_sample,
            ssm_out(ssm_p, 0), ssd_hist_out(hx_p, hbc_p, 0, k_ssd), sc_hist_out(sc_p, 0, k_sc),
            ssm_out(ssm_s, 1), ssd_hist_out(hx_s, hbc_s, 1, k_ssd), sc_hist_out(sc_s, 1, k_sc))
```

```python
import functools

import jax
import jax.numpy as jnp
from jax import lax
from jax.experimental import pallas as pl
from jax.experimental.pallas import tpu as pltpu

EPS = 1e-5
HEAD_DIM = 64
D_STATE = 128
N_GROUPS = 8
LANES = 128
SUBLANES = 8
CHUNK = 128
HALO = SUBLANES
VMEM_LIMIT = 56 * 1024 * 1024
LOG2_E = 1.4426950408889634

f32 = jnp.float32
bf16 = jnp.bfloat16


def _sigmoid(x):
    return 1.0 / (1.0 + jnp.exp(-x))


def _params(*sem):
    return pltpu.CompilerParams(dimension_semantics=sem, vmem_limit_bytes=VMEM_LIMIT)


def _rmsnorm_kernel(x_ref, w_ref, o_ref):
    x = x_ref[...]
    ms = jnp.mean(x * x, axis=-1, keepdims=True)
    o_ref[...] = (x * lax.rsqrt(ms + EPS) * w_ref[...]).astype(o_ref.dtype)


def _rmsnorm(x, w, out_dtype):
    m, d = x.shape
    tm = 256 if m % 256 == 0 else m
    return pl.pallas_call(
        _rmsnorm_kernel,
        grid=(m // tm,),
        in_specs=[pl.BlockSpec((tm, d), lambda i: (i, 0)),
                  pl.BlockSpec((1, d), lambda i: (0, 0))],
        out_specs=pl.BlockSpec((tm, d), lambda i: (i, 0)),
        out_shape=jax.ShapeDtypeStruct((m, d), out_dtype),
        compiler_params=_params("arbitrary"),
        name="rmsnorm",
    )(x, w.reshape(1, d))


def _mm_kernel(x_ref, w_ref, o_ref):
    o_ref[...] = jnp.dot(x_ref[...], w_ref[...], preferred_element_type=f32).astype(o_ref.dtype)


def _mm_res_kernel(x_ref, w_ref, r_ref, o_ref):
    acc = jnp.dot(x_ref[...], w_ref[...], preferred_element_type=f32)
    o_ref[...] = (r_ref[...] + acc).astype(o_ref.dtype)


def _pick(n, candidates):
    for c in candidates:
        if n % c == 0:
            return c
    return n


def _mm_tiles(m, k, n, with_res):
    tm = _pick(m, (1024, 512, 256))
    budget = 44 * 1024 * 1024
    for tn in (1024, 512, 256, 128):
        if n % tn:
            continue
        need = 2 * (tm * k * 2 + k * tn * 2 + tm * tn * 4 * (2 if with_res else 1))
        if need <= budget:
            return tm, tn
    return tm, _pick(n, (128,))


def _matmul(x, w, res=None, out_dtype=f32):
    m, k = x.shape
    n = w.shape[1]
    tm, tn = _mm_tiles(m, k, n, res is not None)
    in_specs = [pl.BlockSpec((tm, k), lambda i, j: (i, 0)),
                pl.BlockSpec((k, tn), lambda i, j: (0, j))]
    args = [x, w]
    kern = _mm_kernel
    if res is not None:
        in_specs.append(pl.BlockSpec((tm, tn), lambda i, j: (i, j)))
        args.append(res)
        kern = _mm_res_kernel
    return pl.pallas_call(
        kern,
        grid=(m // tm, n // tn),
        in_specs=in_specs,
        out_specs=pl.BlockSpec((tm, tn), lambda i, j: (i, j)),
        out_shape=jax.ShapeDtypeStruct((m, n), out_dtype),
        compiler_params=_params("arbitrary", "arbitrary"),
        name="matmul_res" if res is not None else "matmul",
    )(*args)


def _mm_cast_kernel(x_ref, w_ref, o_ref, wb_ref):
    wb = w_ref[...].astype(bf16)
    wb_ref[...] = wb
    o_ref[...] = jnp.dot(x_ref[...], wb, preferred_element_type=f32).astype(o_ref.dtype)


def _mm_cast_res_kernel(x_ref, w_ref, r_ref, o_ref, wb_ref):
    wb = w_ref[...].astype(bf16)
    wb_ref[...] = wb
    acc = jnp.dot(x_ref[...], wb, preferred_element_type=f32)
    o_ref[...] = (r_ref[...] + acc).astype(o_ref.dtype)


def _matmul_cast(x, w32, layer, col0, n, res=None, out_dtype=f32):
    m, k = x.shape
    tile_bytes = 8 * 1024 * 1024
    tn = next(t for t in (512, 256, 128) if n % t == 0 and col0 % t == 0 and k * t * 4 <= tile_bytes)
    j0 = col0 // tn
    in_specs = [pl.BlockSpec((m, k), lambda j: (0, 0)),
                pl.BlockSpec((None, k, tn), lambda j: (layer, 0, j0 + j))]
    args = [x, w32]
    kern = _mm_cast_kernel
    if res is not None:
        in_specs.append(pl.BlockSpec((m, tn), lambda j: (0, j)))
        args.append(res)
        kern = _mm_cast_res_kernel
    return pl.pallas_call(
        kern,
        grid=(n // tn,),
        in_specs=in_specs,
        out_specs=[pl.BlockSpec((m, tn), lambda j: (0, j)),
                   pl.BlockSpec((k, tn), lambda j: (0, j))],
        out_shape=[jax.ShapeDtypeStruct((m, n), out_dtype),
                   jax.ShapeDtypeStruct((k, n), bf16)],
        compiler_params=_params("arbitrary"),
        name="matmul_cast",
    )(*args)


def _shift_matrix(taps, q):
    t = jnp.arange(q)[:, None]
    s = jnp.arange(q)[None, :]
    return jnp.concatenate([(s == t - (taps - 1 - k)) for k in range(taps)], axis=0).astype(bf16)


def _conv_silu(x16, hist, shift, w_ref, b_ref, lo, hi, taps):
    q = x16.shape[0]
    shifted = jnp.dot(shift, x16, preferred_element_type=f32)
    row = lax.broadcasted_iota(jnp.int32, (HALO, x16.shape[1]), 0)
    acc = None
    for k in range(taps):
        d = taps - 1 - k
        part = shifted[k * q:(k + 1) * q]
        if d > 0:
            head = jnp.where(row < d, pltpu.roll(hist, d, 0), part[0:HALO])
            part = jnp.concatenate([head, part[HALO:]], axis=0)
        term = part * w_ref[k:k + 1, lo:hi]
        acc = term if acc is None else acc + term
    acc = acc + b_ref[:, lo:hi]
    return acc * _sigmoid(acc), shifted[(taps - 1) * q:]


def _ssd_kernel(z_ref, x_ref, bc_ref, dt_ref, cxi_ref, cbci_ref, si_ref, shift_ref,
                cwx_ref, cwbc_ref, cbx_ref, cbbc_ref, dtb_ref, alog_ref, dsk_ref, nw_ref,
                y_ref, cxo_ref, cbco_ref, so_ref,
                hx, hbc, st, *, rows, n_chunks, taps):
    q = CHUNK
    gw = HEAD_DIM * (x_ref.shape[2] // HEAD_DIM // N_GROUPS)
    heads_per_group = gw // HEAD_DIM
    c = pl.program_id(1)

    @pl.when(c == 0)
    def _():
        hx[...] = cxi_ref[0]
        hbc[...] = cbci_ref[0]
        st[...] = si_ref[0]

    def pad_rows(v):
        if rows == q:
            return v
        return jnp.concatenate([v, jnp.zeros((q - rows, v.shape[1]), v.dtype)], axis=0)

    row_i = lax.broadcasted_iota(jnp.int32, (q, q), 0)
    col_i = lax.broadcasted_iota(jnp.int32, (q, q), 1)
    causal = row_i >= col_i
    v = pad_rows(dt_ref[0]) + dtb_ref[...]
    dt = jnp.maximum(v, 0.0) + jnp.log1p(jnp.exp(-jnp.abs(v)))
    if rows < q:
        dt = jnp.where(lax.broadcasted_iota(jnp.int32, dt.shape, 0) < rows, dt, 0.0)
    a = dt * (-jnp.exp(alog_ref[...]))
    acs = jnp.dot(causal.astype(f32), a, precision=lax.Precision.HIGHEST,
                  preferred_element_type=f32) * LOG2_E
    acs_t = acs.T
    src_t = acs_t - jnp.log2(dt.T)
    w_t = jnp.exp2(acs_t[:, q - 1:q] - src_t)

    shift = shift_ref[...]
    n_state_cols = D_STATE * N_GROUPS
    bc16 = pad_rows(bc_ref[0])
    b_all, b32 = _conv_silu(bc16[:, 0:n_state_cols], hbc[:, 0:n_state_cols], shift,
                            cwbc_ref, cbbc_ref, 0, n_state_cols, taps)
    c_all, c32 = _conv_silu(bc16[:, n_state_cols:], hbc[:, n_state_cols:], shift,
                            cwbc_ref, cbbc_ref, n_state_cols, 2 * n_state_cols, taps)
    hbc[:, 0:n_state_cols] = b32[rows - HALO:rows]
    hbc[:, n_state_cols:] = c32[rows - HALO:rows]
    lane = lax.broadcasted_iota(jnp.int32, (q, LANES), 1)
    first_head = lane < HEAD_DIM
    neg_inf = jnp.float32(-jnp.inf)

    for g in range(N_GROUPS):
        glo = g * gw
        b_g = b_all[:, g * D_STATE:(g + 1) * D_STATE]
        c_g = c_all[:, g * D_STATE:(g + 1) * D_STATE].astype(bf16)
        cb = lax.dot_general(c_g, b_g.astype(bf16), (((1,), (1,)), ((), ())),
                             preferred_element_type=f32)
        b_t = b_g.T
        s_g = st[:, glo:glo + gw]
        y_off = jnp.dot(c_g, s_g.astype(bf16), preferred_element_type=f32)
        x_g, x32 = _conv_silu(pad_rows(x_ref[0, :, glo:glo + gw]), hx[:, glo:glo + gw], shift,
                              cwx_ref, cbx_ref, glo, glo + gw, taps)
        hx[:, glo:glo + gw] = x32[rows - HALO:rows]

        y_parts = []
        for p in range(heads_per_group // 2):
            plo = p * LANES
            x_p = x_g[:, plo:plo + LANES]
            m_parts, bw_parts, e_parts = [], [], []
            for hh in range(2):
                h = g * heads_per_group + 2 * p + hh
                col_b = jnp.broadcast_to(acs[:, h:h + 1], (q, q))
                seg = col_b - src_t[h:h + 1, :]
                m_parts.append(cb * jnp.exp2(jnp.where(causal, seg, neg_inf)))
                bw_parts.append(b_t * w_t[h:h + 1, :])
                e_parts.append(jnp.exp2(col_b))
            lhs = jnp.concatenate([jnp.concatenate(m_parts, axis=1),
                                   jnp.concatenate(bw_parts, axis=1)], axis=0).astype(bf16)
            x_bd = jnp.concatenate([jnp.where(first_head, x_p, 0.0),
                                    jnp.where(first_head, 0.0, x_p)], axis=0).astype(bf16)
            res = jnp.dot(lhs, x_bd, preferred_element_type=f32)
            e_sel = jnp.where(first_head, e_parts[0], e_parts[1])
            y_parts.append(res[0:q] + y_off[:, plo:plo + LANES] * e_sel
                           + x_p * dsk_ref[:, glo + plo:glo + plo + LANES])
            st[:, glo + plo:glo + plo + LANES] = (s_g[:, plo:plo + LANES] * e_sel[q - 1:q, :]
                                                  + res[q:q + D_STATE])
        y_g = jnp.concatenate(y_parts, axis=1)
        z_g = pad_rows(z_ref[0, :, glo:glo + gw]).astype(f32)
        y_g = y_g * (z_g * _sigmoid(z_g))
        ms = jnp.mean(y_g * y_g, axis=-1, keepdims=True)
        y_n = y_g * lax.rsqrt(ms + EPS) * nw_ref[:, glo:glo + gw]
        y_ref[0, :, glo:glo + gw] = y_n[0:rows].astype(y_ref.dtype)

    @pl.when(c == n_chunks - 1)
    def _():
        cxo_ref[0] = hx[...]
        cbco_ref[0] = hbc[...]
        so_ref[0] = st[...]


def _ssd_core(proj, dt_raw, conv_x, conv_bc, state, cw_x, cw_bc, cb_x, cb_bc, dt_bias, a_log,
              d_skip_wide, norm_w, rows, shared_state):
    bn, length, _ = proj.shape
    d_inner = cw_x.shape[1]
    n_bc = cw_bc.shape[1]
    n_heads = dt_bias.shape[0]
    taps = cw_x.shape[0]
    n_chunks = length // rows
    sidx = (lambda b, c: (0, 0, 0)) if shared_state else (lambda b, c: (b, 0, 0))
    const = lambda b, c: (0, 0)
    kern = functools.partial(_ssd_kernel, rows=rows, n_chunks=n_chunks, taps=taps)
    return pl.pallas_call(
        kern,
        grid=(bn, n_chunks),
        in_specs=[
            pl.BlockSpec((1, rows, d_inner), lambda b, c: (b, c, 0)),
            pl.BlockSpec((1, rows, d_inner), lambda b, c: (b, c, 1)),
            pl.BlockSpec((1, rows, n_bc), lambda b, c: (b, c, 2 * d_inner // n_bc)),
            pl.BlockSpec((1, rows, n_heads), lambda b, c: (b, c, 0)),
            pl.BlockSpec((1, HALO, d_inner), sidx),
            pl.BlockSpec((1, HALO, n_bc), sidx),
            pl.BlockSpec((1, D_STATE, d_inner), sidx),
            pl.BlockSpec((taps * CHUNK, CHUNK), const),
            pl.BlockSpec((taps, d_inner), const),
            pl.BlockSpec((taps, n_bc), const),
            pl.BlockSpec((1, d_inner), const),
            pl.BlockSpec((1, n_bc), const),
            pl.BlockSpec((1, n_heads), const),
            pl.BlockSpec((1, n_heads), const),
            pl.BlockSpec((1, d_inner), const),
            pl.BlockSpec((1, d_inner), const),
        ],
        out_specs=[
            pl.BlockSpec((1, rows, d_inner), lambda b, c: (b, c, 0)),
            pl.BlockSpec((1, HALO, d_inner), lambda b, c: (b, 0, 0)),
            pl.BlockSpec((1, HALO, n_bc), lambda b, c: (b, 0, 0)),
            pl.BlockSpec((1, D_STATE, d_inner), lambda b, c: (b, 0, 0)),
        ],
        out_shape=[
            jax.ShapeDtypeStruct((bn, length, d_inner), bf16),
            jax.ShapeDtypeStruct((bn, HALO, d_inner), f32),
            jax.ShapeDtypeStruct((bn, HALO, n_bc), f32),
            jax.ShapeDtypeStruct((bn, D_STATE, d_inner), f32),
        ],
        scratch_shapes=[
            pltpu.VMEM((HALO, d_inner), f32),
            pltpu.VMEM((HALO, n_bc), f32),
            pltpu.VMEM((D_STATE, d_inner), f32),
        ],
        compiler_params=_params("arbitrary", "arbitrary"),
        name="ssd_core",
    )(proj, proj, proj, dt_raw, conv_x, conv_bc, state, _shift_matrix(taps, CHUNK), cw_x, cw_bc,
      cb_x.reshape(1, -1), cb_bc.reshape(1, -1), dt_bias.reshape(1, -1), a_log.reshape(1, -1),
      d_skip_wide.reshape(1, -1), norm_w.reshape(1, -1))


def _sc_kernel(g_ref, b_ref, c_ref, v_ref, ci_ref, w_ref, y_ref, co_ref, hist, *, rows, n_steps, taps):
    t = pl.program_id(2)

    @pl.when(t == 0)
    def _():
        hist[...] = ci_ref[0]

    cur = c_ref[0].astype(f32) * v_ref[0].astype(f32)
    prev = hist[...]
    row = lax.broadcasted_iota(jnp.int32, prev.shape, 0)
    cv = None
    for k in range(taps):
        d = taps - 1 - k
        part = cur
        if d > 0:
            rolled = pltpu.roll(cur, d, 0)
            head = jnp.where(row < d, pltpu.roll(prev, d, 0), rolled[0:HALO])
            part = jnp.concatenate([head, rolled[HALO:]], axis=0)
        term = part * w_ref[k:k + 1, :]
        cv = term if cv is None else cv + term
    gate = g_ref[0].astype(f32)
    y_ref[0] = ((gate * _sigmoid(gate)) * b_ref[0].astype(f32) * cv).astype(y_ref.dtype)
    hist[...] = cur[rows - HALO:rows]

    @pl.when(t == n_steps - 1)
    def _():
        co_ref[0] = hist[...]


def _sc_core(proj, conv_in, conv_w, rows, shared_state):
    bn, length, four_w = proj.shape
    width = four_w // 4
    taps = conv_w.shape[0]
    tc = _pick(width, (1024, 512, 256, 128))
    nj = width // tc
    n_steps = length // rows
    sidx = (lambda b, j, t: (0, 0, j)) if shared_state else (lambda b, j, t: (b, 0, j))
    kern = functools.partial(_sc_kernel, rows=rows, n_steps=n_steps, taps=taps)
    col = lambda part: (lambda b, j, t: (b, t, part * nj + j))
    return pl.pallas_call(
        kern,
        grid=(bn, nj, n_steps),
        in_specs=[pl.BlockSpec((1, rows, tc), col(0)),
                  pl.BlockSpec((1, rows, tc), col(1)),
                  pl.BlockSpec((1, rows, tc), col(2)),
                  pl.BlockSpec((1, rows, tc), col(3)),
                  pl.BlockSpec((1, HALO, tc), sidx),
                  pl.BlockSpec((taps, tc), lambda b, j, t: (0, j))],
        out_specs=[pl.BlockSpec((1, rows, tc), lambda b, j, t: (b, t, j)),
                   pl.BlockSpec((1, HALO, tc), lambda b, j, t: (b, 0, j))],
        out_shape=[jax.ShapeDtypeStruct((bn, length, width), bf16),
                   jax.ShapeDtypeStruct((bn, HALO, width), f32)],
        scratch_shapes=[pltpu.VMEM((HALO, tc), f32)],
        compiler_params=_params("arbitrary", "arbitrary", "arbitrary"),
        name="sc_core",
    )(proj, proj, proj, proj, conv_in, conv_w)


def _pad_history(buf):
    return jnp.pad(buf, ((0, 0), (HALO - buf.shape[1], 0), (0, 0)))


def _trunk(h, bn, rows, ssm_states, ssd_hist_x, ssd_hist_bc, sc_hist, wts, w16, shared_state):
    m, d_model = h.shape
    length = m // bn
    depth = wts["ln_w"].shape[0]
    d_inner = wts["ssd_w_out"].shape[1]
    n_main = wts["ssd_w_in"].shape[2] - wts["ssd_dt_bias"].shape[1]
    cast = w16 is None
    if cast:
        w16 = {"ssd_w_main": [], "ssd_w_dt": [], "ssd_w_out": [], "sc_w_in": [], "sc_w_out": []}

    def mm(x, name, src, j, col0, n, res=None, out_dtype=f32):
        if cast:
            out, wb = _matmul_cast(x, wts[src], j, col0, n, res=res, out_dtype=out_dtype)
            w16[name].append(wb)
            return out
        return _matmul(x, w16[name][j], res=res, out_dtype=out_dtype)

    new_ssm, new_hx, new_hbc, new_sc = [], [], [], []
    for i in range(depth):
        j = i // 2
        u = _rmsnorm(h, wts["ln_w"][i], bf16)
        if i % 2 == 0:
            proj = mm(u, "ssd_w_main", "ssd_w_in", j, 0, n_main, out_dtype=bf16)
            dt_raw = mm(u, "ssd_w_dt", "ssd_w_in", j, n_main, wts["ssd_w_in"].shape[2] - n_main)
            y, hx, hbc, s = _ssd_core(
                proj.reshape(bn, length, -1), dt_raw.reshape(bn, length, -1),
                ssd_hist_x[j], ssd_hist_bc[j], ssm_states[j],
                wts["ssd_conv_w"][j][:, :d_inner], wts["ssd_conv_w"][j][:, d_inner:],
                wts["ssd_conv_b"][j][:d_inner], wts["ssd_conv_b"][j][d_inner:],
                wts["ssd_dt_bias"][j], wts["ssd_a_log"][j], wts["ssd_d_wide"][j],
                wts["ssd_norm_w"][j], rows, shared_state)
            new_ssm.append(s)
            new_hx.append(hx)
            new_hbc.append(hbc)
            h = mm(y.reshape(m, d_inner), "ssd_w_out", "ssd_w_out", j, 0, d_model, res=h)
        else:
            proj = mm(u, "sc_w_in", "sc_w_in", j, 0, wts["sc_w_in"].shape[2], out_dtype=bf16)
            sc_rows = rows if rows < CHUNK else _pick(length, (512, 256, 128))
            y, hist = _sc_core(proj.reshape(bn, length, -1), sc_hist[j], wts["sc_conv_w"][j],
                               sc_rows, shared_state)
            new_sc.append(hist)
            h = mm(y.reshape(m, -1), "sc_w_out", "sc_w_out", j, 0, d_model, res=h)
    y = _rmsnorm(h, wts["final_norm_w"], f32)
    return y, new_ssm, new_hx, new_hbc, new_sc, w16


def kernel(x_prompt, x_sample, state_ssm, state_ssd_conv, state_sc_conv, meta_tokens, ln_w, ssd_w_in,
           ssd_conv_w, ssd_conv_b, ssd_dt_bias, ssd_a_log, ssd_d_skip, ssd_norm_w, ssd_w_out,
           sc_w_in, sc_conv_w, sc_w_out, final_norm_w):
    batch, seq, d_model = x_prompt.shape
    dec_batch, dec_seq, _ = x_sample.shape
    n_ssd, _, n_heads, head_dim, d_state = state_ssm.shape
    n_sc = state_sc_conv.shape[0]
    d_inner = n_heads * head_dim
    assert head_dim == HEAD_DIM and d_state == D_STATE
    assert meta_tokens.shape[0] == dec_seq and dec_seq % SUBLANES == 0 and dec_seq <= CHUNK
    assert seq % CHUNK == 0

    wts = {
        "ln_w": ln_w, "final_norm_w": final_norm_w,
        "ssd_w_in": ssd_w_in, "ssd_w_out": ssd_w_out, "sc_w_in": sc_w_in, "sc_w_out": sc_w_out,
        "ssd_conv_w": ssd_conv_w, "ssd_conv_b": ssd_conv_b, "ssd_dt_bias": ssd_dt_bias,
        "ssd_a_log": ssd_a_log, "ssd_norm_w": ssd_norm_w, "sc_conv_w": sc_conv_w,
        "ssd_d_wide": jnp.repeat(ssd_d_skip, head_dim, axis=1),
    }

    n_small = 1 + dec_batch
    h_small = jnp.concatenate([meta_tokens[None].astype(x_sample.dtype), x_sample], axis=0)
    zero_lead = lambda a: jnp.concatenate([jnp.zeros((1,) + a.shape[1:], a.dtype), a], axis=0)
    ssm_in, hx_in, hbc_in, sc_in = [], [], [], []
    for j in range(n_ssd):
        s = zero_lead(state_ssm[j]).reshape(n_small, d_inner, d_state)
        ssm_in.append(jnp.swapaxes(s, 1, 2))
        hist = _pad_history(zero_lead(state_ssd_conv[j]))
        hx_in.append(hist[:, :, :d_inner])
        hbc_in.append(hist[:, :, d_inner:])
    for j in range(n_sc):
        sc_in.append(_pad_history(zero_lead(state_sc_conv[j])))

    y_small, ssm_s, hx_s, hbc_s, sc_s, w16 = _trunk(
        h_small.reshape(n_small * dec_seq, d_model), n_small, dec_seq,
        ssm_in, hx_in, hbc_in, sc_in, wts, None, shared_state=False)

    y_p, ssm_p, hx_p, hbc_p, sc_p, _ = _trunk(
        x_prompt.reshape(batch * seq, d_model), batch, CHUNK,
        [s[0:1] for s in ssm_s], [a[0:1] for a in hx_s], [a[0:1] for a in hbc_s],
        [a[0:1] for a in sc_s], wts, w16, shared_state=True)

    def ssm_out(parts, lo):
        return jnp.stack([jnp.swapaxes(s[lo:], 1, 2).reshape(-1, n_heads, head_dim, d_state)
                          for s in parts])

    def ssd_hist_out(px, pbc, lo, k):
        return jnp.stack([jnp.concatenate([a[lo:, HALO - k:], b[lo:, HALO - k:]], axis=-1)
                          for a, b in zip(px, pbc)])

    def sc_hist_out(parts, lo, k):
        return jnp.stack([a[lo:, HALO - k:] for a in parts])

    k_ssd = state_ssd_conv.shape[2]
    k_sc = state_sc_conv.shape[2]
    y_prompt = y_p.reshape(batch, seq, d_model)
    y_sample = y_small.reshape(n_small, dec_seq, d_model)[1:]
    return (y_prompt, y_sample,
            ssm_out(ssm_p, 0), ssd_hist_out(hx_p, hbc_p, 0, k_ssd), sc_hist_out(sc_p, 0, k_sc),
            ssm_out(ssm_s, 1), ssd_hist_out(hx_s, hbc_s, 1, k_ssd), sc_hist_out(sc_s, 1, k_sc))
```

```python
import functools

import jax
import jax.numpy as jnp
from jax import lax
from jax.experimental import pallas as pl
from jax.experimental.pallas import tpu as pltpu

EPS = 1e-5
HEAD_DIM = 64
D_STATE = 128
N_GROUPS = 8
LANES = 128
SUBLANES = 8
CHUNK = 128
HALO = SUBLANES
VMEM_LIMIT = 56 * 1024 * 1024
LOG2_E = 1.4426950408889634

f32 = jnp.float32
bf16 = jnp.bfloat16


def _sigmoid(x):
    return 1.0 / (1.0 + jnp.exp(-x))


def _causal_conv(cur, prev, w):
    taps = w.shape[0]
    row = lax.broadcasted_iota(jnp.int32, prev.shape, 0)
    acc = None
    for k in range(taps):
        d = taps - 1 - k
        part = cur
        if d > 0:
            rolled = pltpu.roll(cur, d, 0)
            head = jnp.where(row < d, pltpu.roll(prev, d, 0), rolled[0:HALO])
            part = jnp.concatenate([head, rolled[HALO:]], axis=0)
        term = part * w[k:k + 1, :]
        acc = term if acc is None else acc + term
    return acc


def _params(*sem):
    return pltpu.CompilerParams(dimension_semantics=sem, vmem_limit_bytes=VMEM_LIMIT)


def _rmsnorm_kernel(x_ref, w_ref, o_ref):
    x = x_ref[...]
    ms = jnp.mean(x * x, axis=-1, keepdims=True)
    o_ref[...] = (x * lax.rsqrt(ms + EPS) * w_ref[...]).astype(o_ref.dtype)


def _rmsnorm(x, w, out_dtype):
    m, d = x.shape
    tm = 256 if m % 256 == 0 else m
    return pl.pallas_call(
        _rmsnorm_kernel,
        grid=(m // tm,),
        in_specs=[pl.BlockSpec((tm, d), lambda i: (i, 0)),
                  pl.BlockSpec((1, d), lambda i: (0, 0))],
        out_specs=pl.BlockSpec((tm, d), lambda i: (i, 0)),
        out_shape=jax.ShapeDtypeStruct((m, d), out_dtype),
        compiler_params=_params("arbitrary"),
        name="rmsnorm",
    )(x, w.reshape(1, d))


def _mm_kernel(x_ref, w_ref, o_ref):
    o_ref[...] = jnp.dot(x_ref[...], w_ref[...], preferred_element_type=f32).astype(o_ref.dtype)


def _mm_res_kernel(x_ref, w_ref, r_ref, o_ref):
    acc = jnp.dot(x_ref[...], w_ref[...], preferred_element_type=f32)
    o_ref[...] = (r_ref[...] + acc).astype(o_ref.dtype)


def _pick(n, candidates):
    for c in candidates:
        if n % c == 0:
            return c
    return n


def _mm_tiles(m, k, n, with_res):
    tm = _pick(m, (1024, 512, 256))
    budget = 44 * 1024 * 1024
    for tn in (1024, 512, 256, 128):
        if n % tn:
            continue
        need = 2 * (tm * k * 2 + k * tn * 2 + tm * tn * 4 * (2 if with_res else 1))
        if need <= budget:
            return tm, tn
    return tm, _pick(n, (128,))


def _matmul(x, w, res=None, out_dtype=f32):
    m, k = x.shape
    n = w.shape[1]
    tm, tn = _mm_tiles(m, k, n, res is not None)
    in_specs = [pl.BlockSpec((tm, k), lambda i, j: (i, 0)),
                pl.BlockSpec((k, tn), lambda i, j: (0, j))]
    args = [x, w]
    kern = _mm_kernel
    if res is not None:
        in_specs.append(pl.BlockSpec((tm, tn), lambda i, j: (i, j)))
        args.append(res)
        kern = _mm_res_kernel
    return pl.pallas_call(
        kern,
        grid=(m // tm, n // tn),
        in_specs=in_specs,
        out_specs=pl.BlockSpec((tm, tn), lambda i, j: (i, j)),
        out_shape=jax.ShapeDtypeStruct((m, n), out_dtype),
        compiler_params=_params("arbitrary", "arbitrary"),
        name="matmul_res" if res is not None else "matmul",
    )(*args)


def _mm_cast_kernel(x_ref, w_ref, o_ref, wb_ref):
    wb = w_ref[...].astype(bf16)
    wb_ref[...] = wb
    o_ref[...] = jnp.dot(x_ref[...], wb, preferred_element_type=f32).astype(o_ref.dtype)


def _mm_cast_res_kernel(x_ref, w_ref, r_ref, o_ref, wb_ref):
    wb = w_ref[...].astype(bf16)
    wb_ref[...] = wb
    acc = jnp.dot(x_ref[...], wb, preferred_element_type=f32)
    o_ref[...] = (r_ref[...] + acc).astype(o_ref.dtype)


def _matmul_cast(x, w32, layer, col0, n, res=None, out_dtype=f32):
    m, k = x.shape
    tile_bytes = 8 * 1024 * 1024
    tn = next(t for t in (512, 256, 128) if n % t == 0 and col0 % t == 0 and k * t * 4 <= tile_bytes)
    j0 = col0 // tn
    in_specs = [pl.BlockSpec((m, k), lambda j: (0, 0)),
                pl.BlockSpec((None, k, tn), lambda j: (layer, 0, j0 + j))]
    args = [x, w32]
    kern = _mm_cast_kernel
    if res is not None:
        in_specs.append(pl.BlockSpec((m, tn), lambda j: (0, j)))
        args.append(res)
        kern = _mm_cast_res_kernel
    return pl.pallas_call(
        kern,
        grid=(n // tn,),
        in_specs=in_specs,
        out_specs=[pl.BlockSpec((m, tn), lambda j: (0, j)),
                   pl.BlockSpec((k, tn), lambda j: (0, j))],
        out_shape=[jax.ShapeDtypeStruct((m, n), out_dtype),
                   jax.ShapeDtypeStruct((k, n), bf16)],
        compiler_params=_params("arbitrary"),
        name="matmul_cast",
    )(*args)


def _gated_conv_proj_kernel(x_ref, wg_ref, wb_ref, wc_ref, wv_ref, cw_ref, hist_ref, o_ref, ho_ref,
                            halo, *, tiles_per_stream):
    i = pl.program_id(0)
    j = pl.program_id(1)

    @pl.when(i % tiles_per_stream == 0)
    def _():
        halo[j] = hist_ref[0]

    x = x_ref[...]
    cur = (jnp.dot(x, wc_ref[...], preferred_element_type=f32)
           * jnp.dot(x, wv_ref[...], preferred_element_type=f32))
    cv = _causal_conv(cur, halo[j], cw_ref[...])
    gate = jnp.dot(x, wg_ref[...], preferred_element_type=f32)
    o_ref[...] = ((gate * _sigmoid(gate)) * jnp.dot(x, wb_ref[...], preferred_element_type=f32)
                  * cv).astype(o_ref.dtype)
    tail = cur[cur.shape[0] - HALO:]
    halo[j] = tail
    ho_ref[0] = tail


def _proj_rows(stream_len):
    return _pick(stream_len, (1024, 512, 256, 128))


def _gated_conv_proj(x, w, conv_w, hist, bn):
    m, k = x.shape
    width = w.shape[1] // 4
    tm, tc = _proj_rows(m // bn), 256
    nj = width // tc
    tiles_per_stream = m // bn // tm
    taps = conv_w.shape[0]
    kern = functools.partial(_gated_conv_proj_kernel, tiles_per_stream=tiles_per_stream)
    wspec = lambda part: pl.BlockSpec((k, tc), lambda i, j: (0, part * nj + j))
    out, tails = pl.pallas_call(
        kern,
        grid=(m // tm, nj),
        in_specs=[pl.BlockSpec((tm, k), lambda i, j: (i, 0)),
                  wspec(0), wspec(1), wspec(2), wspec(3),
                  pl.BlockSpec((taps, tc), lambda i, j: (0, j)),
                  pl.BlockSpec((1, HALO, tc), lambda i, j: (0, 0, j))],
        out_specs=[pl.BlockSpec((tm, tc), lambda i, j: (i, j)),
                   pl.BlockSpec((1, HALO, tc), lambda i, j: (i, 0, j))],
        out_shape=[jax.ShapeDtypeStruct((m, width), bf16),
                   jax.ShapeDtypeStruct((m // tm, HALO, width), f32)],
        scratch_shapes=[pltpu.VMEM((nj, HALO, tc), f32)],
        compiler_params=_params("arbitrary", "arbitrary"),
        name="gated_conv_proj",
    )(x, w, w, w, w, conv_w, hist)
    return out, tails[tiles_per_stream - 1::tiles_per_stream]


def _shift_matrix(taps, q):
    t = jnp.arange(q)[:, None]
    s = jnp.arange(q)[None, :]
    return jnp.concatenate([(s == t - (taps - 1 - k)) for k in range(taps)], axis=0).astype(bf16)


def _conv_silu(x16, hist, shift, w_ref, b_ref, lo, hi, taps):
    q = x16.shape[0]
    shifted = jnp.dot(shift, x16, preferred_element_type=f32)
    row = lax.broadcasted_iota(jnp.int32, (HALO, x16.shape[1]), 0)
    acc = None
    for k in range(taps):
        d = taps - 1 - k
        part = shifted[k * q:(k + 1) * q]
        if d > 0:
            head = jnp.where(row < d, pltpu.roll(hist, d, 0), part[0:HALO])
            part = jnp.concatenate([head, part[HALO:]], axis=0)
        term = part * w_ref[k:k + 1, lo:hi]
        acc = term if acc is None else acc + term
    acc = acc + b_ref[:, lo:hi]
    return acc * _sigmoid(acc), shifted[(taps - 1) * q:]


def _ssd_kernel(z_ref, x_ref, bc_ref, dt_ref, cxi_ref, cbci_ref, si_ref, shift_ref,
                cwx_ref, cwbc_ref, cbx_ref, cbbc_ref, dtb_ref, alog_ref, dsk_ref, nw_ref,
                y_ref, cxo_ref, cbco_ref, so_ref,
                hx, hbc, st, *, rows, n_chunks, taps):
    q = CHUNK
    d_inner = x_ref.shape[2]
    gw = d_inner // N_GROUPS
    heads_per_group = gw // HEAD_DIM
    c = pl.program_id(1)

    @pl.when(c == 0)
    def _():
        hx[...] = cxi_ref[0]
        hbc[...] = cbci_ref[0]
        for blk in range(d_inner // LANES):
            st[:, blk * LANES:(blk + 1) * LANES] = si_ref[0, blk * LANES:(blk + 1) * LANES, :].T

    def pad_rows(v):
        if rows == q:
            return v
        return jnp.concatenate([v, jnp.zeros((q - rows, v.shape[1]), v.dtype)], axis=0)

    row_i = lax.broadcasted_iota(jnp.int32, (q, q), 0)
    col_i = lax.broadcasted_iota(jnp.int32, (q, q), 1)
    causal = row_i >= col_i
    v = pad_rows(dt_ref[0]) + dtb_ref[...]
    dt = jnp.maximum(v, 0.0) + jnp.log1p(jnp.exp(-jnp.abs(v)))
    if rows < q:
        dt = jnp.where(lax.broadcasted_iota(jnp.int32, dt.shape, 0) < rows, dt, 0.0)
    a = dt * (-jnp.exp(alog_ref[...]))
    acs = jnp.dot(causal.astype(f32), a, precision=lax.Precision.HIGHEST,
                  preferred_element_type=f32) * LOG2_E
    acs_t = acs.T
    src_t = acs_t - jnp.log2(dt.T)
    w_t = jnp.exp2(acs_t[:, q - 1:q] - src_t)

    n_state_cols = D_STATE * N_GROUPS
    bc16 = pad_rows(bc_ref[0])
    shift = shift_ref[...]
    b_all, b32 = _conv_silu(bc16[:, 0:n_state_cols], hbc[:, 0:n_state_cols], shift,
                            cwbc_ref, cbbc_ref, 0, n_state_cols, taps)
    c_all, c32 = _conv_silu(bc16[:, n_state_cols:], hbc[:, n_state_cols:], shift,
                            cwbc_ref, cbbc_ref, n_state_cols, 2 * n_state_cols, taps)
    hbc[:, 0:n_state_cols] = b32[rows - HALO:rows]
    hbc[:, n_state_cols:] = c32[rows - HALO:rows]
    lane = lax.broadcasted_iota(jnp.int32, (q, LANES), 1)
    first_head = lane < HEAD_DIM
    neg_inf = jnp.float32(-jnp.inf)

    for g in range(N_GROUPS):
        glo = g * gw
        b_g = b_all[:, g * D_STATE:(g + 1) * D_STATE]
        c_g = c_all[:, g * D_STATE:(g + 1) * D_STATE].astype(bf16)
        cb = lax.dot_general(c_g, b_g.astype(bf16), (((1,), (1,)), ((), ())),
                             preferred_element_type=f32)
        b_t = b_g.T
        s_g = st[:, glo:glo + gw]
        y_off = jnp.dot(c_g, s_g.astype(bf16), preferred_element_type=f32)
        x_g, x32 = _conv_silu(pad_rows(x_ref[0, :, glo:glo + gw]), hx[:, glo:glo + gw], shift,
                              cwx_ref, cbx_ref, glo, glo + gw, taps)
        hx[:, glo:glo + gw] = x32[rows - HALO:rows]

        y_parts = []
        for p in range(heads_per_group // 2):
            plo = p * LANES
            x_p = x_g[:, plo:plo + LANES]
            m_parts, bw_parts, e_parts = [], [], []
            for hh in range(2):
                h = g * heads_per_group + 2 * p + hh
                col_b = jnp.broadcast_to(acs[:, h:h + 1], (q, q))
                seg = col_b - src_t[h:h + 1, :]
                m_parts.append(cb * jnp.exp2(jnp.where(causal, seg, neg_inf)))
                bw_parts.append(b_t * w_t[h:h + 1, :])
                e_parts.append(jnp.exp2(col_b))
            lhs = jnp.concatenate([jnp.concatenate(m_parts, axis=1),
                                   jnp.concatenate(bw_parts, axis=1)], axis=0).astype(bf16)
            x_bd = jnp.concatenate([jnp.where(first_head, x_p, 0.0),
                                    jnp.where(first_head, 0.0, x_p)], axis=0).astype(bf16)
            res = jnp.dot(lhs, x_bd, preferred_element_type=f32)
            e_sel = jnp.where(first_head, e_parts[0], e_parts[1])
            y_parts.append(res[0:q] + y_off[:, plo:plo + LANES] * e_sel
                           + x_p * dsk_ref[:, glo + plo:glo + plo + LANES])
            st[:, glo + plo:glo + plo + LANES] = (s_g[:, plo:plo + LANES] * e_sel[q - 1:q, :]
                                                  + res[q:q + D_STATE])
        y_g = jnp.concatenate(y_parts, axis=1)
        z_g = pad_rows(z_ref[0, :, glo:glo + gw]).astype(f32)
        y_g = y_g * (z_g * _sigmoid(z_g))
        ms = jnp.mean(y_g * y_g, axis=-1, keepdims=True)
        y_n = y_g * lax.rsqrt(ms + EPS) * nw_ref[:, glo:glo + gw]
        y_ref[0, :, glo:glo + gw] = y_n[0:rows].astype(y_ref.dtype)

    @pl.when(c == n_chunks - 1)
    def _():
        cxo_ref[0] = hx[...]
        cbco_ref[0] = hbc[...]
        for blk in range(d_inner // LANES):
            so_ref[0, blk * LANES:(blk + 1) * LANES, :] = st[:, blk * LANES:(blk + 1) * LANES].T


def _ssd_core(proj, dt_raw, conv_x, conv_bc, state, cw_x, cw_bc, cb_x, cb_bc, dt_bias, a_log,
              d_skip_wide, norm_w, rows, shared_state):
    bn, length, _ = proj.shape
    d_inner = cw_x.shape[1]
    n_bc = cw_bc.shape[1]
    n_heads = dt_bias.shape[0]
    taps = cw_x.shape[0]
    n_chunks = length // rows
    sidx = (lambda b, c: (0, 0, 0)) if shared_state else (lambda b, c: (b, 0, 0))
    const = lambda b, c: (0, 0)
    vec = lambda n: pl.BlockSpec((1, n), const)
    kern = functools.partial(_ssd_kernel, rows=rows, n_chunks=n_chunks, taps=taps)
    return pl.pallas_call(
        kern,
        grid=(bn, n_chunks),
        in_specs=[
            pl.BlockSpec((1, rows, d_inner), lambda b, c: (b, c, 0)),
            pl.BlockSpec((1, rows, d_inner), lambda b, c: (b, c, 1)),
            pl.BlockSpec((1, rows, n_bc), lambda b, c: (b, c, 2 * d_inner // n_bc)),
            pl.BlockSpec((1, rows, n_heads), lambda b, c: (b, c, 0)),
            pl.BlockSpec((1, HALO, d_inner), sidx),
            pl.BlockSpec((1, HALO, n_bc), sidx),
            pl.BlockSpec((1, d_inner, D_STATE), sidx),
            pl.BlockSpec((taps * CHUNK, CHUNK), const),
            pl.BlockSpec((taps, d_inner), const),
            pl.BlockSpec((taps, n_bc), const),
            vec(d_inner), vec(n_bc), vec(n_heads), vec(n_heads), vec(d_inner), vec(d_inner),
        ],
        out_specs=[
            pl.BlockSpec((1, rows, d_inner), lambda b, c: (b, c, 0)),
            pl.BlockSpec((1, HALO, d_inner), lambda b, c: (b, 0, 0)),
            pl.BlockSpec((1, HALO, n_bc), lambda b, c: (b, 0, 0)),
            pl.BlockSpec((1, d_inner, D_STATE), lambda b, c: (b, 0, 0)),
        ],
        out_shape=[
            jax.ShapeDtypeStruct((bn, length, d_inner), bf16),
            jax.ShapeDtypeStruct((bn, HALO, d_inner), f32),
            jax.ShapeDtypeStruct((bn, HALO, n_bc), f32),
            jax.ShapeDtypeStruct((bn, d_inner, D_STATE), f32),
        ],
        scratch_shapes=[
            pltpu.VMEM((HALO, d_inner), f32),
            pltpu.VMEM((HALO, n_bc), f32),
            pltpu.VMEM((D_STATE, d_inner), f32),
        ],
        compiler_params=_params("arbitrary", "arbitrary"),
        name="ssd_core",
    )(proj, proj, proj, dt_raw, conv_x, conv_bc, state, _shift_matrix(taps, CHUNK), cw_x, cw_bc,
      cb_x.reshape(1, -1), cb_bc.reshape(1, -1), dt_bias.reshape(1, -1), a_log.reshape(1, -1),
      d_skip_wide.reshape(1, -1), norm_w.reshape(1, -1))


def _sc_kernel(g_ref, b_ref, c_ref, v_ref, ci_ref, w_ref, y_ref, co_ref, hist, *, rows, n_steps, taps):
    t = pl.program_id(2)

    @pl.when(t == 0)
    def _():
        hist[...] = ci_ref[0]

    cur = c_ref[0].astype(f32) * v_ref[0].astype(f32)
    cv = _causal_conv(cur, hist[...], w_ref[...])
    gate = g_ref[0].astype(f32)
    y_ref[0] = ((gate * _sigmoid(gate)) * b_ref[0].astype(f32) * cv).astype(y_ref.dtype)
    hist[...] = cur[rows - HALO:rows]

    @pl.when(t == n_steps - 1)
    def _():
        co_ref[0] = hist[...]


def _sc_core(proj, conv_in, conv_w, rows, shared_state):
    bn, length, four_w = proj.shape
    width = four_w // 4
    taps = conv_w.shape[0]
    tc = _pick(width, (1024, 512, 256, 128))
    nj = width // tc
    n_steps = length // rows
    sidx = (lambda b, j, t: (0, 0, j)) if shared_state else (lambda b, j, t: (b, 0, j))
    kern = functools.partial(_sc_kernel, rows=rows, n_steps=n_steps, taps=taps)
    col = lambda part: (lambda b, j, t: (b, t, part * nj + j))
    return pl.pallas_call(
        kern,
        grid=(bn, nj, n_steps),
        in_specs=[pl.BlockSpec((1, rows, tc), col(0)),
                  pl.BlockSpec((1, rows, tc), col(1)),
                  pl.BlockSpec((1, rows, tc), col(2)),
                  pl.BlockSpec((1, rows, tc), col(3)),
                  pl.BlockSpec((1, HALO, tc), sidx),
                  pl.BlockSpec((taps, tc), lambda b, j, t: (0, j))],
        out_specs=[pl.BlockSpec((1, rows, tc), lambda b, j, t: (b, t, j)),
                   pl.BlockSpec((1, HALO, tc), lambda b, j, t: (b, 0, j))],
        out_shape=[jax.ShapeDtypeStruct((bn, length, width), bf16),
                   jax.ShapeDtypeStruct((bn, HALO, width), f32)],
        scratch_shapes=[pltpu.VMEM((HALO, tc), f32)],
        compiler_params=_params("arbitrary", "arbitrary", "arbitrary"),
        name="sc_core",
    )(proj, proj, proj, proj, conv_in, conv_w)


def _pad_history(buf):
    return jnp.pad(buf, ((0, 0), (HALO - buf.shape[1], 0), (0, 0)))


def _trunk(h, bn, rows, ssm_states, ssd_hist_x, ssd_hist_bc, sc_hist, wts, w16, shared_state):
    m, d_model = h.shape
    length = m // bn
    depth = wts["ln_w"].shape[0]
    d_inner = wts["ssd_w_out"].shape[1]
    n_main = wts["ssd_w_in"].shape[2] - wts["ssd_dt_bias"].shape[1]
    cast = w16 is None
    if cast:
        w16 = {"ssd_w_main": [], "ssd_w_dt": [], "ssd_w_out": [], "sc_w_in": [], "sc_w_out": []}

    def mm(x, name, src, j, col0, n, res=None, out_dtype=f32):
        if cast:
            out, wb = _matmul_cast(x, wts[src], j, col0, n, res=res, out_dtype=out_dtype)
            w16[name].append(wb)
            return out
        return _matmul(x, w16[name][j], res=res, out_dtype=out_dtype)

    new_ssm, new_hx, new_hbc, new_sc = [], [], [], []
    for i in range(depth):
        j = i // 2
        u = _rmsnorm(h, wts["ln_w"][i], bf16)
        if i % 2 == 0:
            proj = mm(u, "ssd_w_main", "ssd_w_in", j, 0, n_main, out_dtype=bf16)
            dt_raw = mm(u, "ssd_w_dt", "ssd_w_in", j, n_main, wts["ssd_w_in"].shape[2] - n_main)
            conv_w, conv_b = wts["ssd_conv_w"][j], wts["ssd_conv_b"][j]
            y, hx, hbc, s = _ssd_core(
                proj.reshape(bn, length, -1), dt_raw.reshape(bn, length, -1),
                ssd_hist_x[j], ssd_hist_bc[j], ssm_states[j],
                conv_w[:, :d_inner], conv_w[:, d_inner:], conv_b[:d_inner], conv_b[d_inner:],
                wts["ssd_dt_bias"][j], wts["ssd_a_log"][j], wts["ssd_d_wide"][j],
                wts["ssd_norm_w"][j], rows, shared_state)
            new_ssm.append(s)
            new_hx.append(hx)
            new_hbc.append(hbc)
            h = mm(y.reshape(m, d_inner), "ssd_w_out", "ssd_w_out", j, 0, d_model, res=h)
        else:
            if cast:
                proj = mm(u, "sc_w_in", "sc_w_in", j, 0, wts["sc_w_in"].shape[2], out_dtype=bf16)
                y, hist = _sc_core(proj.reshape(bn, length, -1), sc_hist[j], wts["sc_conv_w"][j],
                                   rows, shared_state)
            else:
                y, hist = _gated_conv_proj(u, w16["sc_w_in"][j], wts["sc_conv_w"][j], sc_hist[j], bn)
            new_sc.append(hist)
            h = mm(y.reshape(m, -1), "sc_w_out", "sc_w_out", j, 0, d_model, res=h)
    y = _rmsnorm(h, wts["final_norm_w"], f32)
    return y, new_ssm, new_hx, new_hbc, new_sc, w16


def kernel(x_prompt, x_sample, state_ssm, state_ssd_conv, state_sc_conv, meta_tokens, ln_w, ssd_w_in,
           ssd_conv_w, ssd_conv_b, ssd_dt_bias, ssd_a_log, ssd_d_skip, ssd_norm_w, ssd_w_out,
           sc_w_in, sc_conv_w, sc_w_out, final_norm_w):
    batch, seq, d_model = x_prompt.shape
    dec_batch, dec_seq, _ = x_sample.shape
    n_ssd, _, n_heads, head_dim, d_state = state_ssm.shape
    n_sc = state_sc_conv.shape[0]
    d_inner = n_heads * head_dim
    assert head_dim == HEAD_DIM and d_state == D_STATE
    assert meta_tokens.shape[0] == dec_seq and dec_seq % SUBLANES == 0 and dec_seq <= CHUNK
    assert seq % CHUNK == 0

    wts = {
        "ln_w": ln_w, "final_norm_w": final_norm_w,
        "ssd_w_in": ssd_w_in, "ssd_w_out": ssd_w_out, "sc_w_in": sc_w_in, "sc_w_out": sc_w_out,
        "ssd_conv_w": ssd_conv_w, "ssd_conv_b": ssd_conv_b, "ssd_dt_bias": ssd_dt_bias,
        "ssd_a_log": ssd_a_log, "ssd_norm_w": ssd_norm_w, "sc_conv_w": sc_conv_w,
        "ssd_d_wide": jnp.repeat(ssd_d_skip, head_dim, axis=1),
    }

    n_small = 1 + dec_batch
    h_small = jnp.concatenate([meta_tokens[None].astype(x_sample.dtype), x_sample], axis=0)
    zero_lead = lambda a: jnp.concatenate([jnp.zeros((1,) + a.shape[1:], a.dtype), a], axis=0)
    ssm_in, hx_in, hbc_in, sc_in = [], [], [], []
    for j in range(n_ssd):
        ssm_in.append(zero_lead(state_ssm[j]).reshape(n_small, d_inner, d_state))
        hist = _pad_history(zero_lead(state_ssd_conv[j]))
        hx_in.append(hist[:, :, :d_inner])
        hbc_in.append(hist[:, :, d_inner:])
    for j in range(n_sc):
        sc_in.append(_pad_history(zero_lead(state_sc_conv[j])))

    y_small, ssm_s, hx_s, hbc_s, sc_s, w16 = _trunk(
        h_small.reshape(n_small * dec_seq, d_model), n_small, dec_seq,
        ssm_in, hx_in, hbc_in, sc_in, wts, None, shared_state=False)

    y_p, ssm_p, hx_p, hbc_p, sc_p, _ = _trunk(
        x_prompt.reshape(batch * seq, d_model), batch, CHUNK,
        [s[0:1] for s in ssm_s], [a[0:1] for a in hx_s], [a[0:1] for a in hbc_s],
        [a[0:1] for a in sc_s], wts, w16, shared_state=True)

    def ssm_out(parts, lo):
        return jnp.stack([s[lo:].reshape(-1, n_heads, head_dim, d_state) for s in parts])

    def ssd_hist_out(px, pbc, lo, k):
        return jnp.stack([jnp.concatenate([a[lo:, HALO - k:], b[lo:, HALO - k:]], axis=-1)
                          for a, b in zip(px, pbc)])

    def sc_hist_out(parts, lo, k):
        return jnp.stack([a[lo:, HALO - k:] for a in parts])

    k_ssd = state_ssd_conv.shape[2]
    k_sc = state_sc_conv.shape[2]
    y_prompt = y_p.reshape(batch, seq, d_model)
    y_sample = y_small.reshape(n_small, dec_seq, d_model)[1:]
    return (y_prompt, y_sample,
            ssm_out(ssm_p, 0), ssd_hist_out(hx_p, hbc_p, 0, k_ssd), sc_hist_out(sc_p, 0, k_sc),
            ssm_out(ssm_s, 1), ssd_hist_out(hx_s, hbc_s, 1, k_ssd), sc_hist_out(sc_s, 1, k_sc))
```

```python
import functools

import jax
import jax.numpy as jnp
from jax import lax
from jax.experimental import pallas as pl
from jax.experimental.pallas import tpu as pltpu

EPS = 1e-5
HEAD_DIM = 64
D_STATE = 128
N_GROUPS = 8
LANES = 128
SUBLANES = 8
CHUNK = 128
HALO = SUBLANES
VMEM_LIMIT = 56 * 1024 * 1024
LOG2_E = 1.4426950408889634

f32 = jnp.float32
bf16 = jnp.bfloat16


def _sigmoid(x):
    return 1.0 / (1.0 + jnp.exp(-x))


def _causal_conv(cur, prev, w):
    taps = w.shape[0]
    row = lax.broadcasted_iota(jnp.int32, prev.shape, 0)
    acc = None
    for k in range(taps):
        d = taps - 1 - k
        part = cur
        if d > 0:
            rolled = pltpu.roll(cur, d, 0)
            head = jnp.where(row < d, pltpu.roll(prev, d, 0), rolled[0:HALO])
            part = jnp.concatenate([head, rolled[HALO:]], axis=0)
        term = part * w[k:k + 1, :]
        acc = term if acc is None else acc + term
    return acc


def _params(*sem):
    return pltpu.CompilerParams(dimension_semantics=sem, vmem_limit_bytes=VMEM_LIMIT)


def _rmsnorm_kernel(x_ref, w_ref, o_ref):
    x = x_ref[...]
    ms = jnp.mean(x * x, axis=-1, keepdims=True)
    o_ref[...] = (x * lax.rsqrt(ms + EPS) * w_ref[...]).astype(o_ref.dtype)


def _rmsnorm(x, w, out_dtype):
    m, d = x.shape
    tm = 256 if m % 256 == 0 else m
    return pl.pallas_call(
        _rmsnorm_kernel,
        grid=(m // tm,),
        in_specs=[pl.BlockSpec((tm, d), lambda i: (i, 0)),
                  pl.BlockSpec((1, d), lambda i: (0, 0))],
        out_specs=pl.BlockSpec((tm, d), lambda i: (i, 0)),
        out_shape=jax.ShapeDtypeStruct((m, d), out_dtype),
        compiler_params=_params("arbitrary"),
        name="rmsnorm",
    )(x, w.reshape(1, d))


def _row_scale(h16):
    hf = h16.astype(f32)
    return lax.rsqrt(jnp.mean(hf * hf, axis=-1, keepdims=True) + EPS)


def _mm_norm_kernel(h_ref, w_ref, o_ref, r_ref):
    @pl.when(pl.program_id(1) == 0)
    def _():
        r_ref[...] = _row_scale(h_ref[...])

    acc = jnp.dot(h_ref[...], w_ref[...], preferred_element_type=f32)
    o_ref[...] = (acc * r_ref[...]).astype(o_ref.dtype)


def _mm_res_kernel(x_ref, w_ref, r_ref, o_ref, o16_ref):
    out = r_ref[...] + jnp.dot(x_ref[...], w_ref[...], preferred_element_type=f32)
    o_ref[...] = out
    o16_ref[...] = out.astype(bf16)


def _pick(n, candidates):
    for c in candidates:
        if n % c == 0:
            return c
    return n


def _mm_tiles(m, k, n, with_res):
    tm = _pick(m, (1024, 512, 256))
    budget = 48 * 1024 * 1024
    out_bytes = 4 + 4 + 2 if with_res else 4
    for tn in (1024, 512, 256, 128):
        if n % tn:
            continue
        need = 2 * (tm * k * 2 + k * tn * 2 + tm * tn * out_bytes)
        if need <= budget:
            return tm, tn
    return tm, _pick(n, (128,))


def _matmul_norm(h16, w, out_dtype):
    m, k = h16.shape
    n = w.shape[1]
    tm, tn = _mm_tiles(m, k, n, False)
    return pl.pallas_call(
        _mm_norm_kernel,
        grid=(m // tm, n // tn),
        in_specs=[pl.BlockSpec((tm, k), lambda i, j: (i, 0)),
                  pl.BlockSpec((k, tn), lambda i, j: (0, j))],
        out_specs=pl.BlockSpec((tm, tn), lambda i, j: (i, j)),
        out_shape=jax.ShapeDtypeStruct((m, n), out_dtype),
        scratch_shapes=[pltpu.VMEM((tm, 1), f32)],
        compiler_params=_params("arbitrary", "arbitrary"),
        name="matmul_norm",
    )(h16, w)


def _matmul_res(x, w, res):
    m, k = x.shape
    n = w.shape[1]
    tm, tn = _mm_tiles(m, k, n, True)
    tile = pl.BlockSpec((tm, tn), lambda i, j: (i, j))
    return pl.pallas_call(
        _mm_res_kernel,
        grid=(m // tm, n // tn),
        in_specs=[pl.BlockSpec((tm, k), lambda i, j: (i, 0)),
                  pl.BlockSpec((k, tn), lambda i, j: (0, j)),
                  tile],
        out_specs=[tile, tile],
        out_shape=[jax.ShapeDtypeStruct((m, n), f32), jax.ShapeDtypeStruct((m, n), bf16)],
        compiler_params=_params("arbitrary", "arbitrary"),
        name="matmul_res",
    )(x, w, res)


def _mm_cast_norm_kernel(h_ref, w_ref, nw_ref, o_ref, wb_ref):
    wb = (w_ref[...] * nw_ref[...]).astype(bf16)
    wb_ref[...] = wb
    acc = jnp.dot(h_ref[...], wb, preferred_element_type=f32)
    o_ref[...] = (acc * _row_scale(h_ref[...])).astype(o_ref.dtype)


def _mm_cast_res_kernel(x_ref, w_ref, r_ref, o_ref, o16_ref, wb_ref):
    wb = w_ref[...].astype(bf16)
    wb_ref[...] = wb
    out = r_ref[...] + jnp.dot(x_ref[...], wb, preferred_element_type=f32)
    o_ref[...] = out
    o16_ref[...] = out.astype(bf16)


def _cast_tile(k, n, col0):
    tile_bytes = 8 * 1024 * 1024
    return next(t for t in (512, 256, 128) if n % t == 0 and col0 % t == 0 and k * t * 4 <= tile_bytes)


def _matmul_cast_norm(h16, w32, layer, col0, n, norm_w, out_dtype):
    m, k = h16.shape
    tn = _cast_tile(k, n, col0)
    j0 = col0 // tn
    return pl.pallas_call(
        _mm_cast_norm_kernel,
        grid=(n // tn,),
        in_specs=[pl.BlockSpec((m, k), lambda j: (0, 0)),
                  pl.BlockSpec((None, k, tn), lambda j: (layer, 0, j0 + j)),
                  pl.BlockSpec((k, 1), lambda j: (0, 0))],
        out_specs=[pl.BlockSpec((m, tn), lambda j: (0, j)),
                   pl.BlockSpec((k, tn), lambda j: (0, j))],
        out_shape=[jax.ShapeDtypeStruct((m, n), out_dtype),
                   jax.ShapeDtypeStruct((k, n), bf16)],
        compiler_params=_params("arbitrary"),
        name="matmul_cast_norm",
    )(h16, w32, norm_w.reshape(k, 1))


def _matmul_cast_res(x, w32, layer, res):
    m, k = x.shape
    n = w32.shape[2]
    tn = _cast_tile(k, n, 0)
    tile = pl.BlockSpec((m, tn), lambda j: (0, j))
    return pl.pallas_call(
        _mm_cast_res_kernel,
        grid=(n // tn,),
        in_specs=[pl.BlockSpec((m, k), lambda j: (0, 0)),
                  pl.BlockSpec((None, k, tn), lambda j: (layer, 0, j)),
                  tile],
        out_specs=[tile, tile, pl.BlockSpec((k, tn), lambda j: (0, j))],
        out_shape=[jax.ShapeDtypeStruct((m, n), f32), jax.ShapeDtypeStruct((m, n), bf16),
                   jax.ShapeDtypeStruct((k, n), bf16)],
        compiler_params=_params("arbitrary"),
        name="matmul_cast_res",
    )(x, w32, res)


def _gated_conv_proj_kernel(x_ref, wg_ref, wb_ref, wc_ref, wv_ref, cw_ref, hist_ref, o_ref, ho_ref,
                            halo, r_ref, *, tiles_per_stream):
    i = pl.program_id(0)
    j = pl.program_id(1)

    @pl.when(i % tiles_per_stream == 0)
    def _():
        halo[j] = hist_ref[0]

    @pl.when(j == 0)
    def _():
        r_ref[...] = _row_scale(x_ref[...])

    x = x_ref[...]
    r = r_ref[...]
    cur = (jnp.dot(x, wc_ref[...], preferred_element_type=f32)
           * jnp.dot(x, wv_ref[...], preferred_element_type=f32)) * (r * r)
    cv = _causal_conv(cur, halo[j], cw_ref[...])
    gate = jnp.dot(x, wg_ref[...], preferred_element_type=f32) * r
    o_ref[...] = ((gate * _sigmoid(gate)) * (jnp.dot(x, wb_ref[...], preferred_element_type=f32) * r)
                  * cv).astype(o_ref.dtype)
    tail = cur[cur.shape[0] - HALO:]
    halo[j] = tail
    ho_ref[0] = tail


def _proj_rows(stream_len):
    return _pick(stream_len, (1024, 512, 256, 128))


def _gated_conv_proj(x, w, conv_w, hist, bn):
    m, k = x.shape
    width = w.shape[1] // 4
    tm, tc = _proj_rows(m // bn), 256
    nj = width // tc
    tiles_per_stream = m // bn // tm
    taps = conv_w.shape[0]
    kern = functools.partial(_gated_conv_proj_kernel, tiles_per_stream=tiles_per_stream)
    wspec = lambda part: pl.BlockSpec((k, tc), lambda i, j: (0, part * nj + j))
    out, tails = pl.pallas_call(
        kern,
        grid=(m // tm, nj),
        in_specs=[pl.BlockSpec((tm, k), lambda i, j: (i, 0)),
                  wspec(0), wspec(1), wspec(2), wspec(3),
                  pl.BlockSpec((taps, tc), lambda i, j: (0, j)),
                  pl.BlockSpec((1, HALO, tc), lambda i, j: (0, 0, j))],
        out_specs=[pl.BlockSpec((tm, tc), lambda i, j: (i, j)),
                   pl.BlockSpec((1, HALO, tc), lambda i, j: (i, 0, j))],
        out_shape=[jax.ShapeDtypeStruct((m, width), bf16),
                   jax.ShapeDtypeStruct((m // tm, HALO, width), f32)],
        scratch_shapes=[pltpu.VMEM((nj, HALO, tc), f32), pltpu.VMEM((tm, 1), f32)],
        compiler_params=_params("arbitrary", "arbitrary"),
        name="gated_conv_proj",
    )(x, w, w, w, w, conv_w, hist)
    return out, tails[tiles_per_stream - 1::tiles_per_stream]


def _shift_matrix(taps, q):
    t = jnp.arange(q)[:, None]
    s = jnp.arange(q)[None, :]
    return jnp.concatenate([(s == t - (taps - 1 - k)) for k in range(taps)], axis=0).astype(bf16)


def _conv_silu(x16, hist, shift, w_ref, b_ref, lo, hi, taps):
    q = x16.shape[0]
    shifted = jnp.dot(shift, x16, preferred_element_type=f32)
    row = lax.broadcasted_iota(jnp.int32, (HALO, x16.shape[1]), 0)
    acc = None
    for k in range(taps):
        d = taps - 1 - k
        part = shifted[k * q:(k + 1) * q]
        if d > 0:
            head = jnp.where(row < d, pltpu.roll(hist, d, 0), part[0:HALO])
            part = jnp.concatenate([head, part[HALO:]], axis=0)
        term = part * w_ref[k:k + 1, lo:hi]
        acc = term if acc is None else acc + term
    acc = acc + b_ref[:, lo:hi]
    return acc * _sigmoid(acc), shifted[(taps - 1) * q:]


def _ssd_kernel(z_ref, x_ref, bc_ref, dt_ref, cxi_ref, cbci_ref, si_ref, shift_ref,
                cwx_ref, cwbc_ref, cbx_ref, cbbc_ref, dtb_ref, alog_ref, dsk_ref, nw_ref,
                y_ref, cxo_ref, cbco_ref, so_ref,
                hx, hbc, st, *, rows, n_chunks, taps):
    q = CHUNK
    d_inner = x_ref.shape[2]
    gw = d_inner // N_GROUPS
    heads_per_group = gw // HEAD_DIM
    c = pl.program_id(1)

    @pl.when(c == 0)
    def _():
        hx[...] = cxi_ref[0]
        hbc[...] = cbci_ref[0]
        for blk in range(d_inner // LANES):
            st[:, blk * LANES:(blk + 1) * LANES] = si_ref[0, blk * LANES:(blk + 1) * LANES, :].T

    def pad_rows(v):
        if rows == q:
            return v
        return jnp.concatenate([v, jnp.zeros((q - rows, v.shape[1]), v.dtype)], axis=0)

    row_i = lax.broadcasted_iota(jnp.int32, (q, q), 0)
    col_i = lax.broadcasted_iota(jnp.int32, (q, q), 1)
    causal = row_i >= col_i
    v = pad_rows(dt_ref[0]) + dtb_ref[...]
    dt = jnp.maximum(v, 0.0) + jnp.log1p(jnp.exp(-jnp.abs(v)))
    if rows < q:
        dt = jnp.where(lax.broadcasted_iota(jnp.int32, dt.shape, 0) < rows, dt, 0.0)
    a = dt * (-jnp.exp(alog_ref[...]))
    acs = jnp.dot(causal.astype(f32), a, precision=lax.Precision.HIGHEST,
                  preferred_element_type=f32) * LOG2_E
    acs_t = acs.T
    src_t = acs_t - jnp.log2(dt.T)
    w_t = jnp.exp2(acs_t[:, q - 1:q] - src_t)

    n_state_cols = D_STATE * N_GROUPS
    bc16 = pad_rows(bc_ref[0])
    shift = shift_ref[...]
    b_all, b32 = _conv_silu(bc16[:, 0:n_state_cols], hbc[:, 0:n_state_cols], shift,
                            cwbc_ref, cbbc_ref, 0, n_state_cols, taps)
    c_all, c32 = _conv_silu(bc16[:, n_state_cols:], hbc[:, n_state_cols:], shift,
                            cwbc_ref, cbbc_ref, n_state_cols, 2 * n_state_cols, taps)
    hbc[:, 0:n_state_cols] = b32[rows - HALO:rows]
    hbc[:, n_state_cols:] = c32[rows - HALO:rows]
    lane = lax.broadcasted_iota(jnp.int32, (q, LANES), 1)
    first_head = lane < HEAD_DIM
    neg_inf = jnp.float32(-jnp.inf)

    for g in range(N_GROUPS):
        glo = g * gw
        b_g = b_all[:, g * D_STATE:(g + 1) * D_STATE]
        c_g = c_all[:, g * D_STATE:(g + 1) * D_STATE].astype(bf16)
        cb = lax.dot_general(c_g, b_g.astype(bf16), (((1,), (1,)), ((), ())),
                             preferred_element_type=f32)
        b_t = b_g.T
        s_g = st[:, glo:glo + gw]
        y_off = jnp.dot(c_g, s_g.astype(bf16), preferred_element_type=f32)
        x_g, x32 = _conv_silu(pad_rows(x_ref[0, :, glo:glo + gw]), hx[:, glo:glo + gw], shift,
                              cwx_ref, cbx_ref, glo, glo + gw, taps)
        hx[:, glo:glo + gw] = x32[rows - HALO:rows]

        y_parts = []
        for p in range(heads_per_group // 2):
            plo = p * LANES
            x_p = x_g[:, plo:plo + LANES]
            m_parts, bw_parts, e_parts = [], [], []
            for hh in range(2):
                h = g * heads_per_group + 2 * p + hh
                col_b = jnp.broadcast_to(acs[:, h:h + 1], (q, q))
                seg = col_b - src_t[h:h + 1, :]
                m_parts.append(cb * jnp.exp2(jnp.where(causal, seg, neg_inf)))
                bw_parts.append(b_t * w_t[h:h + 1, :])
                e_parts.append(jnp.exp2(col_b))
            lhs = jnp.concatenate([jnp.concatenate(m_parts, axis=1),
                                   jnp.concatenate(bw_parts, axis=1)], axis=0).astype(bf16)
            x_bd = jnp.concatenate([jnp.where(first_head, x_p, 0.0),
                                    jnp.where(first_head, 0.0, x_p)], axis=0).astype(bf16)
            res = jnp.dot(lhs, x_bd, preferred_element_type=f32)
            e_sel = jnp.where(first_head, e_parts[0], e_parts[1])
            y_parts.append(res[0:q] + y_off[:, plo:plo + LANES] * e_sel
                           + x_p * dsk_ref[:, glo + plo:glo + plo + LANES])
            st[:, glo + plo:glo + plo + LANES] = (s_g[:, plo:plo + LANES] * e_sel[q - 1:q, :]
                                                  + res[q:q + D_STATE])
        y_g = jnp.concatenate(y_parts, axis=1)
        z_g = pad_rows(z_ref[0, :, glo:glo + gw]).astype(f32)
        y_g = y_g * (z_g * _sigmoid(z_g))
        ms = jnp.mean(y_g * y_g, axis=-1, keepdims=True)
        y_n = y_g * lax.rsqrt(ms + EPS) * nw_ref[:, glo:glo + gw]
        y_ref[0, :, glo:glo + gw] = y_n[0:rows].astype(y_ref.dtype)

    @pl.when(c == n_chunks - 1)
    def _():
        cxo_ref[0] = hx[...]
        cbco_ref[0] = hbc[...]
        for blk in range(d_inner // LANES):
            so_ref[0, blk * LANES:(blk + 1) * LANES, :] = st[:, blk * LANES:(blk + 1) * LANES].T


def _ssd_core(proj, dt_raw, conv_x, conv_bc, state, cw_x, cw_bc, cb_x, cb_bc, dt_bias, a_log,
              d_skip_wide, norm_w, rows, shared_state):
    bn, length, _ = proj.shape
    d_inner = cw_x.shape[1]
    n_bc = cw_bc.shape[1]
    n_heads = dt_bias.shape[0]
    taps = cw_x.shape[0]
    n_chunks = length // rows
    sidx = (lambda b, c: (0, 0, 0)) if shared_state else (lambda b, c: (b, 0, 0))
    const = lambda b, c: (0, 0)
    vec = lambda n: pl.BlockSpec((1, n), const)
    kern = functools.partial(_ssd_kernel, rows=rows, n_chunks=n_chunks, taps=taps)
    return pl.pallas_call(
        kern,
        grid=(bn, n_chunks),
        in_specs=[
            pl.BlockSpec((1, rows, d_inner), lambda b, c: (b, c, 0)),
            pl.BlockSpec((1, rows, d_inner), lambda b, c: (b, c, 1)),
            pl.BlockSpec((1, rows, n_bc), lambda b, c: (b, c, 2 * d_inner // n_bc)),
            pl.BlockSpec((1, rows, n_heads), lambda b, c: (b, c, 0)),
            pl.BlockSpec((1, HALO, d_inner), sidx),
            pl.BlockSpec((1, HALO, n_bc), sidx),
            pl.BlockSpec((1, d_inner, D_STATE), sidx),
            pl.BlockSpec((taps * CHUNK, CHUNK), const),
            pl.BlockSpec((taps, d_inner), const),
            pl.BlockSpec((taps, n_bc), const),
            vec(d_inner), vec(n_bc), vec(n_heads), vec(n_heads), vec(d_inner), vec(d_inner),
        ],
        out_specs=[
            pl.BlockSpec((1, rows, d_inner), lambda b, c: (b, c, 0)),
            pl.BlockSpec((1, HALO, d_inner), lambda b, c: (b, 0, 0)),
            pl.BlockSpec((1, HALO, n_bc), lambda b, c: (b, 0, 0)),
            pl.BlockSpec((1, d_inner, D_STATE), lambda b, c: (b, 0, 0)),
        ],
        out_shape=[
            jax.ShapeDtypeStruct((bn, length, d_inner), bf16),
            jax.ShapeDtypeStruct((bn, HALO, d_inner), f32),
            jax.ShapeDtypeStruct((bn, HALO, n_bc), f32),
            jax.ShapeDtypeStruct((bn, d_inner, D_STATE), f32),
        ],
        scratch_shapes=[
            pltpu.VMEM((HALO, d_inner), f32),
            pltpu.VMEM((HALO, n_bc), f32),
            pltpu.VMEM((D_STATE, d_inner), f32),
        ],
        compiler_params=_params("arbitrary", "arbitrary"),
        name="ssd_core",
    )(proj, proj, proj, dt_raw, conv_x, conv_bc, state, _shift_matrix(taps, CHUNK), cw_x, cw_bc,
      cb_x.reshape(1, -1), cb_bc.reshape(1, -1), dt_bias.reshape(1, -1), a_log.reshape(1, -1),
      d_skip_wide.reshape(1, -1), norm_w.reshape(1, -1))


def _sc_kernel(g_ref, b_ref, c_ref, v_ref, ci_ref, w_ref, y_ref, co_ref, hist, *, rows, n_steps, taps):
    t = pl.program_id(2)

    @pl.when(t == 0)
    def _():
        hist[...] = ci_ref[0]

    cur = c_ref[0].astype(f32) * v_ref[0].astype(f32)
    cv = _causal_conv(cur, hist[...], w_ref[...])
    gate = g_ref[0].astype(f32)
    y_ref[0] = ((gate * _sigmoid(gate)) * b_ref[0].astype(f32) * cv).astype(y_ref.dtype)
    hist[...] = cur[rows - HALO:rows]

    @pl.when(t == n_steps - 1)
    def _():
        co_ref[0] = hist[...]


def _sc_core(proj, conv_in, conv_w, rows, shared_state):
    bn, length, four_w = proj.shape
    width = four_w // 4
    taps = conv_w.shape[0]
    tc = _pick(width, (1024, 512, 256, 128))
    nj = width // tc
    n_steps = length // rows
    sidx = (lambda b, j, t: (0, 0, j)) if shared_state else (lambda b, j, t: (b, 0, j))
    kern = functools.partial(_sc_kernel, rows=rows, n_steps=n_steps, taps=taps)
    col = lambda part: (lambda b, j, t: (b, t, part * nj + j))
    return pl.pallas_call(
        kern,
        grid=(bn, nj, n_steps),
        in_specs=[pl.BlockSpec((1, rows, tc), col(0)),
                  pl.BlockSpec((1, rows, tc), col(1)),
                  pl.BlockSpec((1, rows, tc), col(2)),
                  pl.BlockSpec((1, rows, tc), col(3)),
                  pl.BlockSpec((1, HALO, tc), sidx),
                  pl.BlockSpec((taps, tc), lambda b, j, t: (0, j))],
        out_specs=[pl.BlockSpec((1, rows, tc), lambda b, j, t: (b, t, j)),
                   pl.BlockSpec((1, HALO, tc), lambda b, j, t: (b, 0, j))],
        out_shape=[jax.ShapeDtypeStruct((bn, length, width), bf16),
                   jax.ShapeDtypeStruct((bn, HALO, width), f32)],
        scratch_shapes=[pltpu.VMEM((HALO, tc), f32)],
        compiler_params=_params("arbitrary", "arbitrary", "arbitrary"),
        name="sc_core",
    )(proj, proj, proj, proj, conv_in, conv_w)


def _pad_history(buf):
    return jnp.pad(buf, ((0, 0), (HALO - buf.shape[1], 0), (0, 0)))


def _trunk(h, bn, rows, ssm_states, ssd_hist_x, ssd_hist_bc, sc_hist, wts, w16, shared_state):
    m, d_model = h.shape
    length = m // bn
    depth = wts["ln_w"].shape[0]
    d_inner = wts["ssd_w_out"].shape[1]
    n_main = wts["ssd_w_in"].shape[2] - wts["ssd_dt_bias"].shape[1]
    cast = w16 is None
    if cast:
        w16 = {"ssd_w_main": [], "ssd_w_dt": [], "ssd_w_out": [], "sc_w_in": [], "sc_w_out": []}

    def in_proj(name, src, j, col0, n, norm_w, out_dtype):
        if cast:
            out, wb = _matmul_cast_norm(h16, wts[src], j, col0, n, norm_w, out_dtype)
            w16[name].append(wb)
            return out
        return _matmul_norm(h16, w16[name][j], out_dtype)

    def out_proj(y, name, j):
        if cast:
            out, out16, wb = _matmul_cast_res(y, wts[name], j, h)
            w16[name].append(wb)
            return out, out16
        return _matmul_res(y, w16[name][j], h)

    h16 = h.astype(bf16)
    new_ssm, new_hx, new_hbc, new_sc = [], [], [], []
    for i in range(depth):
        j = i // 2
        norm_w = wts["ln_w"][i]
        if i % 2 == 0:
            proj = in_proj("ssd_w_main", "ssd_w_in", j, 0, n_main, norm_w, bf16)
            dt_raw = in_proj("ssd_w_dt", "ssd_w_in", j, n_main, wts["ssd_w_in"].shape[2] - n_main,
                             norm_w, f32)
            conv_w, conv_b = wts["ssd_conv_w"][j], wts["ssd_conv_b"][j]
            y, hx, hbc, s = _ssd_core(
                proj.reshape(bn, length, -1), dt_raw.reshape(bn, length, -1),
                ssd_hist_x[j], ssd_hist_bc[j], ssm_states[j],
                conv_w[:, :d_inner], conv_w[:, d_inner:], conv_b[:d_inner], conv_b[d_inner:],
                wts["ssd_dt_bias"][j], wts["ssd_a_log"][j], wts["ssd_d_wide"][j],
                wts["ssd_norm_w"][j], rows, shared_state)
            new_ssm.append(s)
            new_hx.append(hx)
            new_hbc.append(hbc)
            h, h16 = out_proj(y.reshape(m, d_inner), "ssd_w_out", j)
        else:
            if cast:
                proj = in_proj("sc_w_in", "sc_w_in", j, 0, wts["sc_w_in"].shape[2], norm_w, bf16)
                y, hist = _sc_core(proj.reshape(bn, length, -1), sc_hist[j], wts["sc_conv_w"][j],
                                   rows, shared_state)
            else:
                y, hist = _gated_conv_proj(h16, w16["sc_w_in"][j], wts["sc_conv_w"][j], sc_hist[j], bn)
            new_sc.append(hist)
            h, h16 = out_proj(y.reshape(m, -1), "sc_w_out", j)
    y = _rmsnorm(h, wts["final_norm_w"], f32)
    return y, new_ssm, new_hx, new_hbc, new_sc, w16


def kernel(x_prompt, x_sample, state_ssm, state_ssd_conv, state_sc_conv, meta_tokens, ln_w, ssd_w_in,
           ssd_conv_w, ssd_conv_b, ssd_dt_bias, ssd_a_log, ssd_d_skip, ssd_norm_w, ssd_w_out,
           sc_w_in, sc_conv_w, sc_w_out, final_norm_w):
    batch, seq, d_model = x_prompt.shape
    dec_batch, dec_seq, _ = x_sample.shape
    n_ssd, _, n_heads, head_dim, d_state = state_ssm.shape
    n_sc = state_sc_conv.shape[0]
    d_inner = n_heads * head_dim
    assert head_dim == HEAD_DIM and d_state == D_STATE
    assert meta_tokens.shape[0] == dec_seq and dec_seq % SUBLANES == 0 and dec_seq <= CHUNK
    assert seq % CHUNK == 0

    wts = {
        "ln_w": ln_w, "final_norm_w": final_norm_w,
        "ssd_w_in": ssd_w_in, "ssd_w_out": ssd_w_out, "sc_w_in": sc_w_in, "sc_w_out": sc_w_out,
        "ssd_conv_w": ssd_conv_w, "ssd_conv_b": ssd_conv_b, "ssd_dt_bias": ssd_dt_bias,
        "ssd_a_log": ssd_a_log, "ssd_norm_w": ssd_norm_w, "sc_conv_w": sc_conv_w,
        "ssd_d_wide": jnp.repeat(ssd_d_skip, head_dim, axis=1),
    }

    n_small = 1 + dec_batch
    h_small = jnp.concatenate([meta_tokens[None].astype(x_sample.dtype), x_sample], axis=0)
    zero_lead = lambda a: jnp.concatenate([jnp.zeros((1,) + a.shape[1:], a.dtype), a], axis=0)
    ssm_in, hx_in, hbc_in, sc_in = [], [], [], []
    for j in range(n_ssd):
        ssm_in.append(zero_lead(state_ssm[j]).reshape(n_small, d_inner, d_state))
        hist = _pad_history(zero_lead(state_ssd_conv[j]))
        hx_in.append(hist[:, :, :d_inner])
        hbc_in.append(hist[:, :, d_inner:])
    for j in range(n_sc):
        sc_in.append(_pad_history(zero_lead(state_sc_conv[j])))

    y_small, ssm_s, hx_s, hbc_s, sc_s, w16 = _trunk(
        h_small.reshape(n_small * dec_seq, d_model), n_small, dec_seq,
        ssm_in, hx_in, hbc_in, sc_in, wts, None, shared_state=False)

    y_p, ssm_p, hx_p, hbc_p, sc_p, _ = _trunk(
        x_prompt.reshape(batch * seq, d_model), batch, CHUNK,
        [s[0:1] for s in ssm_s], [a[0:1] for a in hx_s], [a[0:1] for a in hbc_s],
        [a[0:1] for a in sc_s], wts, w16, shared_state=True)

    def ssm_out(parts, lo):
        return jnp.stack([s[lo:].reshape(-1, n_heads, head_dim, d_state) for s in parts])

    def ssd_hist_out(px, pbc, lo, k):
        return jnp.stack([jnp.concatenate([a[lo:, HALO - k:], b[lo:, HALO - k:]], axis=-1)
                          for a, b in zip(px, pbc)])

    def sc_hist_out(parts, lo, k):
        return jnp.stack([a[lo:, HALO - k:] for a in parts])

    k_ssd = state_ssd_conv.shape[2]
    k_sc = state_sc_conv.shape[2]
    y_prompt = y_p.reshape(batch, seq, d_model)
    y_sample = y_small.reshape(n_small, dec_seq, d_model)[1:]
    return (y_prompt, y_sample,
            ssm_out(ssm_p, 0), ssd_hist_out(hx_p, hbc_p, 0, k_ssd), sc_hist_out(sc_p, 0, k_sc),
            ssm_out(ssm_s, 1), ssd_hist_out(hx_s, hbc_s, 1, k_ssd), sc_hist_out(sc_s, 1, k_sc))
```

```python
import functools

import jax
import jax.numpy as jnp
from jax import lax
from jax.experimental import pallas as pl
from jax.experimental.pallas import tpu as pltpu

EPS = 1e-5
HEAD_DIM = 64
D_STATE = 128
N_GROUPS = 8
LANES = 128
SUBLANES = 8
CHUNK = 128
HALO = SUBLANES
VMEM_LIMIT = 56 * 1024 * 1024
LOG2_E = 1.4426950408889634

f32 = jnp.float32
bf16 = jnp.bfloat16


def _sigmoid(x):
    return 1.0 / (1.0 + jnp.exp2(x * (-LOG2_E)))


def _causal_conv(cur, prev, w):
    taps = w.shape[0]
    row = lax.broadcasted_iota(jnp.int32, prev.shape, 0)
    acc = None
    for k in range(taps):
        d = taps - 1 - k
        part = cur
        if d > 0:
            rolled = pltpu.roll(cur, d, 0)
            head = jnp.where(row < d, pltpu.roll(prev, d, 0), rolled[0:HALO])
            part = jnp.concatenate([head, rolled[HALO:]], axis=0)
        term = part * w[k:k + 1, :]
        acc = term if acc is None else acc + term
    return acc


def _params(*sem):
    return pltpu.CompilerParams(dimension_semantics=sem, vmem_limit_bytes=VMEM_LIMIT)


def _rmsnorm_kernel(x_ref, w_ref, o_ref):
    x = x_ref[...]
    ms = jnp.mean(x * x, axis=-1, keepdims=True)
    o_ref[...] = (x * lax.rsqrt(ms + EPS) * w_ref[...]).astype(o_ref.dtype)


def _rmsnorm(x, w, out_dtype):
    m, d = x.shape
    tm = 256 if m % 256 == 0 else m
    return pl.pallas_call(
        _rmsnorm_kernel,
        grid=(m // tm,),
        in_specs=[pl.BlockSpec((tm, d), lambda i: (i, 0)),
                  pl.BlockSpec((1, d), lambda i: (0, 0))],
        out_specs=pl.BlockSpec((tm, d), lambda i: (i, 0)),
        out_shape=jax.ShapeDtypeStruct((m, d), out_dtype),
        compiler_params=_params("arbitrary"),
        name="rmsnorm",
    )(x, w.reshape(1, d))


def _row_scale(h16):
    hf = h16.astype(f32)
    return lax.rsqrt(jnp.mean(hf * hf, axis=-1, keepdims=True) + EPS)


def _mm_norm_kernel(h_ref, w_ref, o_ref, r_ref):
    @pl.when(pl.program_id(1) == 0)
    def _():
        r_ref[...] = _row_scale(h_ref[...])

    acc = jnp.dot(h_ref[...], w_ref[...], preferred_element_type=f32)
    o_ref[...] = (acc * r_ref[...]).astype(o_ref.dtype)


def _mm_res_kernel(x_ref, w_ref, r_ref, o_ref, o16_ref):
    out = r_ref[...] + jnp.dot(x_ref[...], w_ref[...], preferred_element_type=f32)
    o_ref[...] = out
    o16_ref[...] = out.astype(bf16)


def _pick(n, candidates):
    for c in candidates:
        if n % c == 0:
            return c
    return n


def _mm_tiles(m, k, n, with_res):
    tm = _pick(m, (1024, 512, 256))
    budget = 48 * 1024 * 1024
    out_bytes = 4 + 4 + 2 if with_res else 4
    for tn in (1024, 512, 256, 128):
        if n % tn:
            continue
        need = 2 * (tm * k * 2 + k * tn * 2 + tm * tn * out_bytes)
        if need <= budget:
            return tm, tn
    return tm, _pick(n, (128,))


def _matmul_norm(h16, w, out_dtype):
    m, k = h16.shape
    n = w.shape[1]
    tm, tn = _mm_tiles(m, k, n, False)
    return pl.pallas_call(
        _mm_norm_kernel,
        grid=(m // tm, n // tn),
        in_specs=[pl.BlockSpec((tm, k), lambda i, j: (i, 0)),
                  pl.BlockSpec((k, tn), lambda i, j: (0, j))],
        out_specs=pl.BlockSpec((tm, tn), lambda i, j: (i, j)),
        out_shape=jax.ShapeDtypeStruct((m, n), out_dtype),
        scratch_shapes=[pltpu.VMEM((tm, 1), f32)],
        compiler_params=_params("arbitrary", "arbitrary"),
        name="matmul_norm",
    )(h16, w)


def _matmul_res(x, w, res):
    m, k = x.shape
    n = w.shape[1]
    tm, tn = _mm_tiles(m, k, n, True)
    tile = pl.BlockSpec((tm, tn), lambda i, j: (i, j))
    return pl.pallas_call(
        _mm_res_kernel,
        grid=(m // tm, n // tn),
        in_specs=[pl.BlockSpec((tm, k), lambda i, j: (i, 0)),
                  pl.BlockSpec((k, tn), lambda i, j: (0, j)),
                  tile],
        out_specs=[tile, tile],
        out_shape=[jax.ShapeDtypeStruct((m, n), f32), jax.ShapeDtypeStruct((m, n), bf16)],
        compiler_params=_params("arbitrary", "arbitrary"),
        name="matmul_res",
    )(x, w, res)


def _mm_cast_norm_kernel(h_ref, w_ref, nw_ref, o_ref, wb_ref):
    wb = (w_ref[...] * nw_ref[...]).astype(bf16)
    wb_ref[...] = wb
    acc = jnp.dot(h_ref[...], wb, preferred_element_type=f32)
    o_ref[...] = (acc * _row_scale(h_ref[...])).astype(o_ref.dtype)


def _mm_cast_res_kernel(x_ref, w_ref, r_ref, o_ref, o16_ref, wb_ref):
    wb = w_ref[...].astype(bf16)
    wb_ref[...] = wb
    out = r_ref[...] + jnp.dot(x_ref[...], wb, preferred_element_type=f32)
    o_ref[...] = out
    o16_ref[...] = out.astype(bf16)


def _cast_tile(k, n, col0):
    tile_bytes = 8 * 1024 * 1024
    return next(t for t in (512, 256, 128) if n % t == 0 and col0 % t == 0 and k * t * 4 <= tile_bytes)


def _matmul_cast_norm(h16, w32, layer, col0, n, norm_w, out_dtype):
    m, k = h16.shape
    tn = _cast_tile(k, n, col0)
    j0 = col0 // tn
    return pl.pallas_call(
        _mm_cast_norm_kernel,
        grid=(n // tn,),
        in_specs=[pl.BlockSpec((m, k), lambda j: (0, 0)),
                  pl.BlockSpec((None, k, tn), lambda j: (layer, 0, j0 + j)),
                  pl.BlockSpec((k, 1), lambda j: (0, 0))],
        out_specs=[pl.BlockSpec((m, tn), lambda j: (0, j)),
                   pl.BlockSpec((k, tn), lambda j: (0, j))],
        out_shape=[jax.ShapeDtypeStruct((m, n), out_dtype),
                   jax.ShapeDtypeStruct((k, n), bf16)],
        compiler_params=_params("arbitrary"),
        name="matmul_cast_norm",
    )(h16, w32, norm_w.reshape(k, 1))


def _matmul_cast_res(x, w32, layer, res):
    m, k = x.shape
    n = w32.shape[2]
    tn = _cast_tile(k, n, 0)
    tile = pl.BlockSpec((m, tn), lambda j: (0, j))
    return pl.pallas_call(
        _mm_cast_res_kernel,
        grid=(n // tn,),
        in_specs=[pl.BlockSpec((m, k), lambda j: (0, 0)),
                  pl.BlockSpec((None, k, tn), lambda j: (layer, 0, j)),
                  tile],
        out_specs=[tile, tile, pl.BlockSpec((k, tn), lambda j: (0, j))],
        out_shape=[jax.ShapeDtypeStruct((m, n), f32), jax.ShapeDtypeStruct((m, n), bf16),
                   jax.ShapeDtypeStruct((k, n), bf16)],
        compiler_params=_params("arbitrary"),
        name="matmul_cast_res",
    )(x, w32, res)


def _gated_conv_proj_kernel(x_ref, wg_ref, wb_ref, wc_ref, wv_ref, cw_ref, hist_ref, o_ref, ho_ref,
                            halo, r_ref, *, tiles_per_stream):
    i = pl.program_id(0)
    j = pl.program_id(1)

    @pl.when(i % tiles_per_stream == 0)
    def _():
        halo[j] = hist_ref[0]

    @pl.when(j == 0)
    def _():
        r_ref[...] = _row_scale(x_ref[...])

    x = x_ref[...]
    r = r_ref[...]
    cur = (jnp.dot(x, wc_ref[...], preferred_element_type=f32)
           * jnp.dot(x, wv_ref[...], preferred_element_type=f32)) * (r * r)
    cv = _causal_conv(cur, halo[j], cw_ref[...])
    gate = jnp.dot(x, wg_ref[...], preferred_element_type=f32) * r
    o_ref[...] = ((gate * _sigmoid(gate)) * (jnp.dot(x, wb_ref[...], preferred_element_type=f32) * r)
                  * cv).astype(o_ref.dtype)
    tail = cur[cur.shape[0] - HALO:]
    halo[j] = tail
    ho_ref[0] = tail


def _proj_rows(stream_len):
    return _pick(stream_len, (1024, 512, 256, 128))


def _gated_conv_proj(x, w, conv_w, hist, bn):
    m, k = x.shape
    width = w.shape[1] // 4
    tm, tc = _proj_rows(m // bn), 256
    nj = width // tc
    tiles_per_stream = m // bn // tm
    taps = conv_w.shape[0]
    kern = functools.partial(_gated_conv_proj_kernel, tiles_per_stream=tiles_per_stream)
    wspec = lambda part: pl.BlockSpec((k, tc), lambda i, j: (0, part * nj + j))
    out, tails = pl.pallas_call(
        kern,
        grid=(m // tm, nj),
        in_specs=[pl.BlockSpec((tm, k), lambda i, j: (i, 0)),
                  wspec(0), wspec(1), wspec(2), wspec(3),
                  pl.BlockSpec((taps, tc), lambda i, j: (0, j)),
                  pl.BlockSpec((1, HALO, tc), lambda i, j: (0, 0, j))],
        out_specs=[pl.BlockSpec((tm, tc), lambda i, j: (i, j)),
                   pl.BlockSpec((1, HALO, tc), lambda i, j: (i, 0, j))],
        out_shape=[jax.ShapeDtypeStruct((m, width), bf16),
                   jax.ShapeDtypeStruct((m // tm, HALO, width), f32)],
        scratch_shapes=[pltpu.VMEM((nj, HALO, tc), f32), pltpu.VMEM((tm, 1), f32)],
        compiler_params=_params("arbitrary", "arbitrary"),
        name="gated_conv_proj",
    )(x, w, w, w, w, conv_w, hist)
    return out, tails[tiles_per_stream - 1::tiles_per_stream]


def _shift_matrix(taps, q):
    t = jnp.arange(q)[:, None]
    s = jnp.arange(q)[None, :]
    return jnp.concatenate([(s == t - (taps - 1 - k)) for k in range(taps)], axis=1).astype(bf16)


def _conv_silu(x16, hist, shift, w_ref, b_ref, lo, hi, taps, rows):
    w = w_ref[:, lo:hi]
    w16 = w.astype(bf16)
    scaled = jnp.concatenate([x16 * w16[k:k + 1, :] for k in range(taps)], axis=0)
    acc = jnp.dot(shift, scaled, preferred_element_type=f32)
    row = lax.broadcasted_iota(jnp.int32, hist.shape, 0)
    head = acc[0:HALO]
    for k in range(taps - 1):
        d = taps - 1 - k
        head = head + jnp.where(row < d, pltpu.roll(hist, d, 0), 0.0) * w[k:k + 1, :]
    acc = jnp.concatenate([head, acc[HALO:]], axis=0) + b_ref[:, lo:hi]
    tail = x16[rows - 2 * HALO:rows].astype(f32)[HALO:]
    return acc * _sigmoid(acc), tail


def _ssd_kernel(z_ref, x_ref, bc_ref, dt_ref, cxi_ref, cbci_ref, si_ref, shift_ref,
                cwx_ref, cwbc_ref, cbx_ref, cbbc_ref, dtb_ref, alog_ref, dsk_ref, nw_ref,
                y_ref, cxo_ref, cbco_ref, so_ref,
                hx, hbc, st, *, rows, n_chunks, taps, first_stream_empty):
    q = CHUNK
    d_inner = x_ref.shape[2]
    gw = d_inner // N_GROUPS
    heads_per_group = gw // HEAD_DIM
    c = pl.program_id(1)

    @pl.when(c == 0)
    def _():
        hx[...] = cxi_ref[0]
        hbc[...] = cbci_ref[0]
        for blk in range(d_inner // LANES):
            st[:, blk * LANES:(blk + 1) * LANES] = si_ref[0, blk * LANES:(blk + 1) * LANES, :].T

    if first_stream_empty:
        @pl.when((c == 0) & (pl.program_id(0) == 0))
        def _():
            st[...] = jnp.zeros_like(st)

    def pad_rows(v):
        if rows == q:
            return v
        return jnp.concatenate([v, jnp.zeros((q - rows, v.shape[1]), v.dtype)], axis=0)

    row_i = lax.broadcasted_iota(jnp.int32, (q, q), 0)
    col_i = lax.broadcasted_iota(jnp.int32, (q, q), 1)
    causal = row_i >= col_i
    v = pad_rows(dt_ref[0]) + dtb_ref[...]
    dt = jnp.maximum(v, 0.0) + jnp.log1p(jnp.exp(-jnp.abs(v)))
    if rows < q:
        dt = jnp.where(lax.broadcasted_iota(jnp.int32, dt.shape, 0) < rows, dt, 0.0)
    a = dt * (-jnp.exp(alog_ref[...]))
    acs = jnp.dot(causal.astype(f32), a, precision=lax.Precision.HIGHEST,
                  preferred_element_type=f32) * LOG2_E
    acs_t = acs.T
    src_t = acs_t - jnp.log2(dt.T)
    w_t = jnp.exp2(acs_t[:, q - 1:q] - src_t)

    n_state_cols = D_STATE * N_GROUPS
    bc16 = pad_rows(bc_ref[0])
    shift = shift_ref[...]
    b_all, b_tail = _conv_silu(bc16[:, 0:n_state_cols], hbc[:, 0:n_state_cols], shift,
                               cwbc_ref, cbbc_ref, 0, n_state_cols, taps, rows)
    c_all, c_tail = _conv_silu(bc16[:, n_state_cols:], hbc[:, n_state_cols:], shift,
                               cwbc_ref, cbbc_ref, n_state_cols, 2 * n_state_cols, taps, rows)
    hbc[:, 0:n_state_cols] = b_tail
    hbc[:, n_state_cols:] = c_tail
    lane = lax.broadcasted_iota(jnp.int32, (q, LANES), 1)
    first_head = lane < HEAD_DIM
    neg_inf = jnp.float32(-jnp.inf)

    for g in range(N_GROUPS):
        glo = g * gw
        b_g = b_all[:, g * D_STATE:(g + 1) * D_STATE]
        c_g = c_all[:, g * D_STATE:(g + 1) * D_STATE].astype(bf16)
        cb = lax.dot_general(c_g, b_g.astype(bf16), (((1,), (1,)), ((), ())),
                             preferred_element_type=f32)
        b_t = b_g.T
        s_g = st[:, glo:glo + gw]
        y_off = jnp.dot(c_g, s_g.astype(bf16), preferred_element_type=f32)
        x_g, x_tail = _conv_silu(pad_rows(x_ref[0, :, glo:glo + gw]), hx[:, glo:glo + gw], shift,
                                 cwx_ref, cbx_ref, glo, glo + gw, taps, rows)
        hx[:, glo:glo + gw] = x_tail

        y_parts = []
        for p in range(heads_per_group // 2):
            plo = p * LANES
            x_p = x_g[:, plo:plo + LANES]
            m_parts, bw_parts, e_parts = [], [], []
            for hh in range(2):
                h = g * heads_per_group + 2 * p + hh
                col_b = jnp.broadcast_to(acs[:, h:h + 1], (q, q))
                seg = col_b - src_t[h:h + 1, :]
                m_parts.append(cb * jnp.exp2(jnp.where(causal, seg, neg_inf)))
                bw_parts.append(b_t * w_t[h:h + 1, :])
                e_parts.append(jnp.exp2(col_b))
            lhs = jnp.concatenate([jnp.concatenate(m_parts, axis=1),
                                   jnp.concatenate(bw_parts, axis=1)], axis=0).astype(bf16)
            x_bd = jnp.concatenate([jnp.where(first_head, x_p, 0.0),
                                    jnp.where(first_head, 0.0, x_p)], axis=0).astype(bf16)
            res = jnp.dot(lhs, x_bd, preferred_element_type=f32)
            e_sel = jnp.where(first_head, e_parts[0], e_parts[1])
            y_parts.append(res[0:q] + y_off[:, plo:plo + LANES] * e_sel
                           + x_p * dsk_ref[:, glo + plo:glo + plo + LANES])
            st[:, glo + plo:glo + plo + LANES] = (s_g[:, plo:plo + LANES] * e_sel[q - 1:q, :]
                                                  + res[q:q + D_STATE])
        y_g = jnp.concatenate(y_parts, axis=1)
        z_g = pad_rows(z_ref[0, :, glo:glo + gw]).astype(f32)
        y_g = y_g * (z_g * _sigmoid(z_g))
        ms = jnp.mean(y_g * y_g, axis=-1, keepdims=True)
        y_n = y_g * lax.rsqrt(ms + EPS) * nw_ref[:, glo:glo + gw]
        y_ref[0, :, glo:glo + gw] = y_n[0:rows].astype(y_ref.dtype)

    @pl.when(c == n_chunks - 1)
    def _():
        cxo_ref[0] = hx[...]
        cbco_ref[0] = hbc[...]
        for blk in range(d_inner // LANES):
            so_ref[0, blk * LANES:(blk + 1) * LANES, :] = st[:, blk * LANES:(blk + 1) * LANES].T


def _ssd_core(proj, dt_raw, conv_x, conv_bc, state, cw_x, cw_bc, cb_x, cb_bc, dt_bias, a_log,
              d_skip_wide, norm_w, rows, shared_state):
    bn, length, _ = proj.shape
    d_inner = cw_x.shape[1]
    n_bc = cw_bc.shape[1]
    n_heads = dt_bias.shape[0]
    taps = cw_x.shape[0]
    n_chunks = length // rows
    first_stream_empty = not shared_state and state.shape[0] == bn - 1
    sidx = (lambda b, c: (0, 0, 0)) if shared_state else (lambda b, c: (b, 0, 0))
    ssm_idx = (lambda b, c: (jnp.maximum(b - 1, 0), 0, 0)) if first_stream_empty else sidx
    const = lambda b, c: (0, 0)
    vec = lambda n: pl.BlockSpec((1, n), const)
    kern = functools.partial(_ssd_kernel, rows=rows, n_chunks=n_chunks, taps=taps,
                             first_stream_empty=first_stream_empty)
    return pl.pallas_call(
        kern,
        grid=(bn, n_chunks),
        in_specs=[
            pl.BlockSpec((1, rows, d_inner), lambda b, c: (b, c, 0)),
            pl.BlockSpec((1, rows, d_inner), lambda b, c: (b, c, 1)),
            pl.BlockSpec((1, rows, n_bc), lambda b, c: (b, c, 2 * d_inner // n_bc)),
            pl.BlockSpec((1, rows, n_heads), lambda b, c: (b, c, 0)),
            pl.BlockSpec((1, HALO, d_inner), sidx),
            pl.BlockSpec((1, HALO, n_bc), sidx),
            pl.BlockSpec((1, d_inner, D_STATE), ssm_idx),
            pl.BlockSpec((CHUNK, taps * CHUNK), const),
            pl.BlockSpec((taps, d_inner), const),
            pl.BlockSpec((taps, n_bc), const),
            vec(d_inner), vec(n_bc), vec(n_heads), vec(n_heads), vec(d_inner), vec(d_inner),
        ],
        out_specs=[
            pl.BlockSpec((1, rows, d_inner), lambda b, c: (b, c, 0)),
            pl.BlockSpec((1, HALO, d_inner), lambda b, c: (b, 0, 0)),
            pl.BlockSpec((1, HALO, n_bc), lambda b, c: (b, 0, 0)),
            pl.BlockSpec((1, d_inner, D_STATE), lambda b, c: (b, 0, 0)),
        ],
        out_shape=[
            jax.ShapeDtypeStruct((bn, length, d_inner), bf16),
            jax.ShapeDtypeStruct((bn, HALO, d_inner), f32),
            jax.ShapeDtypeStruct((bn, HALO, n_bc), f32),
            jax.ShapeDtypeStruct((bn, d_inner, D_STATE), f32),
        ],
        scratch_shapes=[
            pltpu.VMEM((HALO, d_inner), f32),
            pltpu.VMEM((HALO, n_bc), f32),
            pltpu.VMEM((D_STATE, d_inner), f32),
        ],
        compiler_params=_params("arbitrary", "arbitrary"),
        name="ssd_core",
    )(proj, proj, proj, dt_raw, conv_x, conv_bc, state, _shift_matrix(taps, CHUNK), cw_x, cw_bc,
      cb_x.reshape(1, -1), cb_bc.reshape(1, -1), dt_bias.reshape(1, -1), a_log.reshape(1, -1),
      d_skip_wide.reshape(1, -1), norm_w.reshape(1, -1))


def _sc_kernel(g_ref, b_ref, c_ref, v_ref, ci_ref, w_ref, y_ref, co_ref, hist, *, rows, n_steps, taps):
    t = pl.program_id(2)

    @pl.when(t == 0)
    def _():
        hist[...] = ci_ref[0]

    cur = c_ref[0].astype(f32) * v_ref[0].astype(f32)
    cv = _causal_conv(cur, hist[...], w_ref[...])
    gate = g_ref[0].astype(f32)
    y_ref[0] = ((gate * _sigmoid(gate)) * b_ref[0].astype(f32) * cv).astype(y_ref.dtype)
    hist[...] = cur[rows - HALO:rows]

    @pl.when(t == n_steps - 1)
    def _():
        co_ref[0] = hist[...]


def _sc_core(proj, conv_in, conv_w, rows, shared_state):
    bn, length, four_w = proj.shape
    width = four_w // 4
    taps = conv_w.shape[0]
    tc = _pick(width, (1024, 512, 256, 128))
    nj = width // tc
    n_steps = length // rows
    sidx = (lambda b, j, t: (0, 0, j)) if shared_state else (lambda b, j, t: (b, 0, j))
    kern = functools.partial(_sc_kernel, rows=rows, n_steps=n_steps, taps=taps)
    col = lambda part: (lambda b, j, t: (b, t, part * nj + j))
    return pl.pallas_call(
        kern,
        grid=(bn, nj, n_steps),
        in_specs=[pl.BlockSpec((1, rows, tc), col(0)),
                  pl.BlockSpec((1, rows, tc), col(1)),
                  pl.BlockSpec((1, rows, tc), col(2)),
                  pl.BlockSpec((1, rows, tc), col(3)),
                  pl.BlockSpec((1, HALO, tc), sidx),
                  pl.BlockSpec((taps, tc), lambda b, j, t: (0, j))],
        out_specs=[pl.BlockSpec((1, rows, tc), lambda b, j, t: (b, t, j)),
                   pl.BlockSpec((1, HALO, tc), lambda b, j, t: (b, 0, j))],
        out_shape=[jax.ShapeDtypeStruct((bn, length, width), bf16),
                   jax.ShapeDtypeStruct((bn, HALO, width), f32)],
        scratch_shapes=[pltpu.VMEM((HALO, tc), f32)],
        compiler_params=_params("arbitrary", "arbitrary", "arbitrary"),
        name="sc_core",
    )(proj, proj, proj, proj, conv_in, conv_w)


def _pad_history(buf):
    return jnp.pad(buf, ((0, 0), (HALO - buf.shape[1], 0), (0, 0)))


def _trunk(h, bn, rows, ssm_states, ssd_hist_x, ssd_hist_bc, sc_hist, wts, w16, shared_state):
    m, d_model = h.shape
    length = m // bn
    depth = wts["ln_w"].shape[0]
    d_inner = wts["ssd_w_out"].shape[1]
    n_main = wts["ssd_w_in"].shape[2] - wts["ssd_dt_bias"].shape[1]
    cast = w16 is None
    if cast:
        w16 = {"ssd_w_main": [], "ssd_w_dt": [], "ssd_w_out": [], "sc_w_in": [], "sc_w_out": []}

    def in_proj(name, src, j, col0, n, norm_w, out_dtype):
        if cast:
            out, wb = _matmul_cast_norm(h16, wts[src], j, col0, n, norm_w, out_dtype)
            w16[name].append(wb)
            return out
        return _matmul_norm(h16, w16[name][j], out_dtype)

    def out_proj(y, name, j):
        if cast:
            out, out16, wb = _matmul_cast_res(y, wts[name], j, h)
            w16[name].append(wb)
            return out, out16
        return _matmul_res(y, w16[name][j], h)

    h16 = h.astype(bf16)
    new_ssm, new_hx, new_hbc, new_sc = [], [], [], []
    for i in range(depth):
        j = i // 2
        norm_w = wts["ln_w"][i]
        if i % 2 == 0:
            proj = in_proj("ssd_w_main", "ssd_w_in", j, 0, n_main, norm_w, bf16)
            dt_raw = in_proj("ssd_w_dt", "ssd_w_in", j, n_main, wts["ssd_w_in"].shape[2] - n_main,
                             norm_w, f32)
            conv_w, conv_b = wts["ssd_conv_w"][j], wts["ssd_conv_b"][j]
            y, hx, hbc, s = _ssd_core(
                proj.reshape(bn, length, -1), dt_raw.reshape(bn, length, -1),
                ssd_hist_x[j], ssd_hist_bc[j], ssm_states[j],
                conv_w[:, :d_inner], conv_w[:, d_inner:], conv_b[:d_inner], conv_b[d_inner:],
                wts["ssd_dt_bias"][j], wts["ssd_a_log"][j], wts["ssd_d_wide"][j],
                wts["ssd_norm_w"][j], rows, shared_state)
            new_ssm.append(s)
            new_hx.append(hx)
            new_hbc.append(hbc)
            h, h16 = out_proj(y.reshape(m, d_inner), "ssd_w_out", j)
        else:
            if cast:
                proj = in_proj("sc_w_in", "sc_w_in", j, 0, wts["sc_w_in"].shape[2], norm_w, bf16)
                y, hist = _sc_core(proj.reshape(bn, length, -1), sc_hist[j], wts["sc_conv_w"][j],
                                   rows, shared_state)
            else:
                y, hist = _gated_conv_proj(h16, w16["sc_w_in"][j], wts["sc_conv_w"][j], sc_hist[j], bn)
            new_sc.append(hist)
            h, h16 = out_proj(y.reshape(m, -1), "sc_w_out", j)
    y = _rmsnorm(h, wts["final_norm_w"], f32)
    return y, new_ssm, new_hx, new_hbc, new_sc, w16


def kernel(x_prompt, x_sample, state_ssm, state_ssd_conv, state_sc_conv, meta_tokens, ln_w, ssd_w_in,
           ssd_conv_w, ssd_conv_b, ssd_dt_bias, ssd_a_log, ssd_d_skip, ssd_norm_w, ssd_w_out,
           sc_w_in, sc_conv_w, sc_w_out, final_norm_w):
    batch, seq, d_model = x_prompt.shape
    dec_batch, dec_seq, _ = x_sample.shape
    n_ssd, _, n_heads, head_dim, d_state = state_ssm.shape
    n_sc = state_sc_conv.shape[0]
    d_inner = n_heads * head_dim
    assert head_dim == HEAD_DIM and d_state == D_STATE
    assert meta_tokens.shape[0] == dec_seq and dec_seq % (2 * SUBLANES) == 0 and dec_seq <= CHUNK
    assert seq % CHUNK == 0

    wts = {
        "ln_w": ln_w, "final_norm_w": final_norm_w,
        "ssd_w_in": ssd_w_in, "ssd_w_out": ssd_w_out, "sc_w_in": sc_w_in, "sc_w_out": sc_w_out,
        "ssd_conv_w": ssd_conv_w, "ssd_conv_b": ssd_conv_b, "ssd_dt_bias": ssd_dt_bias,
        "ssd_a_log": ssd_a_log, "ssd_norm_w": ssd_norm_w, "sc_conv_w": sc_conv_w,
        "ssd_d_wide": jnp.repeat(ssd_d_skip, head_dim, axis=1),
    }

    n_small = 1 + dec_batch
    h_small = jnp.concatenate([meta_tokens[None].astype(x_sample.dtype), x_sample], axis=0)
    zero_lead = lambda a: jnp.concatenate([jnp.zeros((1,) + a.shape[1:], a.dtype), a], axis=0)
    ssm_in, hx_in, hbc_in, sc_in = [], [], [], []
    for j in range(n_ssd):
        ssm_in.append(state_ssm[j].reshape(dec_batch, d_inner, d_state))
        hist = _pad_history(zero_lead(state_ssd_conv[j]))
        hx_in.append(hist[:, :, :d_inner])
        hbc_in.append(hist[:, :, d_inner:])
    for j in range(n_sc):
        sc_in.append(_pad_history(zero_lead(state_sc_conv[j])))

    y_small, ssm_s, hx_s, hbc_s, sc_s, w16 = _trunk(
        h_small.reshape(n_small * dec_seq, d_model), n_small, dec_seq,
        ssm_in, hx_in, hbc_in, sc_in, wts, None, shared_state=False)

    y_p, ssm_p, hx_p, hbc_p, sc_p, _ = _trunk(
        x_prompt.reshape(batch * seq, d_model), batch, CHUNK,
        [s[0:1] for s in ssm_s], [a[0:1] for a in hx_s], [a[0:1] for a in hbc_s],
        [a[0:1] for a in sc_s], wts, w16, shared_state=True)

    def ssm_out(parts, lo):
        return jnp.stack([s[lo:].reshape(-1, n_heads, head_dim, d_state) for s in parts])

    def ssd_hist_out(px, pbc, lo, k):
        return jnp.stack([jnp.concatenate([a[lo:, HALO - k:], b[lo:, HALO - k:]], axis=-1)
                          for a, b in zip(px, pbc)])

    def sc_hist_out(parts, lo, k):
        return jnp.stack([a[lo:, HALO - k:] for a in parts])

    k_ssd = state_ssd_conv.shape[2]
    k_sc = state_sc_conv.shape[2]
    y_prompt = y_p.reshape(batch, seq, d_model)
    y_sample = y_small.reshape(n_small, dec_seq, d_model)[1:]
    return (y_prompt, y_sample,
            ssm_out(ssm_p, 0), ssd_hist_out(hx_p, hbc_p, 0, k_ssd), sc_hist_out(sc_p, 0, k_sc),
            ssm_out(ssm_s, 1), ssd_hist_out(hx_s, hbc_s, 1, k_ssd), sc_hist_out(sc_s, 1, k_sc))
```

```python
import functools

import jax
import jax.numpy as jnp
from jax import lax
from jax.experimental import pallas as pl
from jax.experimental.pallas import tpu as pltpu

EPS = 1e-5
HEAD_DIM = 64
D_STATE = 128
N_GROUPS = 8
LANES = 128
SUBLANES = 8
CHUNK = 128
HALO = SUBLANES
VMEM_LIMIT = 56 * 1024 * 1024
LOG2_E = 1.4426950408889634

f32 = jnp.float32
bf16 = jnp.bfloat16


def _sigmoid(x):
    return 1.0 / (1.0 + jnp.exp2(x * (-LOG2_E)))


def _causal_conv(cur, prev, w):
    taps = w.shape[0]
    row = lax.broadcasted_iota(jnp.int32, prev.shape, 0)
    acc = None
    for k in range(taps):
        d = taps - 1 - k
        part = cur
        if d > 0:
            rolled = pltpu.roll(cur, d, 0)
            head = jnp.where(row < d, pltpu.roll(prev, d, 0), rolled[0:HALO])
            part = jnp.concatenate([head, rolled[HALO:]], axis=0)
        term = part * w[k:k + 1, :]
        acc = term if acc is None else acc + term
    return acc


def _params(*sem):
    return pltpu.CompilerParams(dimension_semantics=sem, vmem_limit_bytes=VMEM_LIMIT)


def _rmsnorm_kernel(x_ref, w_ref, o_ref):
    x = x_ref[...]
    ms = jnp.mean(x * x, axis=-1, keepdims=True)
    o_ref[...] = (x * lax.rsqrt(ms + EPS) * w_ref[...]).astype(o_ref.dtype)


def _rmsnorm(x, w, out_dtype):
    m, d = x.shape
    tm = 256 if m % 256 == 0 else m
    return pl.pallas_call(
        _rmsnorm_kernel,
        grid=(m // tm,),
        in_specs=[pl.BlockSpec((tm, d), lambda i: (i, 0)),
                  pl.BlockSpec((1, d), lambda i: (0, 0))],
        out_specs=pl.BlockSpec((tm, d), lambda i: (i, 0)),
        out_shape=jax.ShapeDtypeStruct((m, d), out_dtype),
        compiler_params=_params("arbitrary"),
        name="rmsnorm",
    )(x, w.reshape(1, d))


def _row_scale(h16):
    hf = h16.astype(f32)
    return lax.rsqrt(jnp.mean(hf * hf, axis=-1, keepdims=True) + EPS)


def _mm_norm_kernel(h_ref, w_ref, o_ref, r_ref):
    @pl.when(pl.program_id(1) == 0)
    def _():
        r_ref[...] = _row_scale(h_ref[...])

    acc = jnp.dot(h_ref[...], w_ref[...], preferred_element_type=f32)
    o_ref[...] = (acc * r_ref[...]).astype(o_ref.dtype)


def _mm_res_kernel(x_ref, w_ref, r_ref, *refs, side):
    n_in = sum(2 if has_norm else 1 for has_norm, _ in side)
    o_ref, o16_ref = refs[n_in], refs[n_in + 1]
    out = r_ref[...] + jnp.dot(x_ref[...], w_ref[...], preferred_element_type=f32)
    o_ref[...] = out
    o16_ref[...] = out.astype(bf16)
    src, dst = 0, n_in + 2
    for has_norm, splits in side:
        block = refs[src][...]
        if has_norm:
            block = block * refs[src + 1][...]
        src += 2 if has_norm else 1
        for lo, hi in splits:
            refs[dst][...] = block[:, lo:hi].astype(bf16)
            dst += 1


def _pick(n, candidates):
    for c in candidates:
        if n % c == 0:
            return c
    return n


def _mm_tiles(m, k, n, with_res):
    tm = _pick(m, (1024, 512, 256))
    budget = 48 * 1024 * 1024
    out_bytes = 4 + 4 + 2 if with_res else 4
    for tn in (1024, 512, 256, 128):
        if n % tn:
            continue
        need = 2 * (tm * k * 2 + k * tn * 2 + tm * tn * out_bytes)
        if need <= budget:
            return tm, tn
    return tm, _pick(n, (128,))


def _matmul_norm(h16, w, out_dtype):
    m, k = h16.shape
    n = w.shape[1]
    tm, tn = _mm_tiles(m, k, n, False)
    return pl.pallas_call(
        _mm_norm_kernel,
        grid=(m // tm, n // tn),
        in_specs=[pl.BlockSpec((tm, k), lambda i, j: (i, 0)),
                  pl.BlockSpec((k, tn), lambda i, j: (0, j))],
        out_specs=pl.BlockSpec((tm, tn), lambda i, j: (i, j)),
        out_shape=jax.ShapeDtypeStruct((m, n), out_dtype),
        scratch_shapes=[pltpu.VMEM((tm, 1), f32)],
        compiler_params=_params("arbitrary", "arbitrary"),
        name="matmul_norm",
    )(h16, w)


def _matmul_res(x, w, res, casts=()):
    m, k = x.shape
    n = w.shape[1]
    tm, tn = _mm_tiles(m, k, n, True)
    nj = n // tn
    n_steps = (m // tm) * nj
    tile = pl.BlockSpec((tm, tn), lambda i, j: (i, j))
    in_specs = [pl.BlockSpec((tm, k), lambda i, j: (i, 0)),
                pl.BlockSpec((k, tn), lambda i, j: (0, j)),
                tile]
    args = [x, w, res]
    out_specs = [tile, tile]
    out_shape = [jax.ShapeDtypeStruct((m, n), f32), jax.ShapeDtypeStruct((m, n), bf16)]
    side = []
    for w32, layer, norm_w, splits in casts:
        rows, cols = w32.shape[1:]
        rb = rows // n_steps
        assert rb * n_steps == rows and rb % (2 * SUBLANES) == 0
        in_specs.append(pl.BlockSpec((None, rb, cols), lambda i, j, layer=layer: (layer, i * nj + j, 0)))
        args.append(w32)
        if norm_w is not None:
            in_specs.append(pl.BlockSpec((rb, 1), lambda i, j: (i * nj + j, 0)))
            args.append(norm_w.reshape(rows, 1))
        for lo, hi in splits:
            out_specs.append(pl.BlockSpec((rb, hi - lo), lambda i, j: (i * nj + j, 0)))
            out_shape.append(jax.ShapeDtypeStruct((rows, hi - lo), bf16))
        side.append((norm_w is not None, tuple(splits)))
    return pl.pallas_call(
        functools.partial(_mm_res_kernel, side=tuple(side)),
        grid=(m // tm, nj),
        in_specs=in_specs,
        out_specs=out_specs,
        out_shape=out_shape,
        compiler_params=_params("arbitrary", "arbitrary"),
        name="matmul_res",
    )(*args)


def _mm_cast_norm_kernel(h_ref, w_ref, nw_ref, o_ref, wb_ref):
    wb = (w_ref[...] * nw_ref[...]).astype(bf16)
    wb_ref[...] = wb
    acc = jnp.dot(h_ref[...], wb, preferred_element_type=f32)
    o_ref[...] = (acc * _row_scale(h_ref[...])).astype(o_ref.dtype)


def _mm_cast_res_kernel(x_ref, w_ref, r_ref, o_ref, o16_ref, wb_ref):
    wb = w_ref[...].astype(bf16)
    wb_ref[...] = wb
    out = r_ref[...] + jnp.dot(x_ref[...], wb, preferred_element_type=f32)
    o_ref[...] = out
    o16_ref[...] = out.astype(bf16)


def _cast_tile(k, n, col0):
    tile_bytes = 8 * 1024 * 1024
    return next(t for t in (512, 256, 128) if n % t == 0 and col0 % t == 0 and k * t * 4 <= tile_bytes)


def _matmul_cast_norm(h16, w32, layer, col0, n, norm_w, out_dtype):
    m, k = h16.shape
    tn = _cast_tile(k, n, col0)
    j0 = col0 // tn
    return pl.pallas_call(
        _mm_cast_norm_kernel,
        grid=(n // tn,),
        in_specs=[pl.BlockSpec((m, k), lambda j: (0, 0)),
                  pl.BlockSpec((None, k, tn), lambda j: (layer, 0, j0 + j)),
                  pl.BlockSpec((k, 1), lambda j: (0, 0))],
        out_specs=[pl.BlockSpec((m, tn), lambda j: (0, j)),
                   pl.BlockSpec((k, tn), lambda j: (0, j))],
        out_shape=[jax.ShapeDtypeStruct((m, n), out_dtype),
                   jax.ShapeDtypeStruct((k, n), bf16)],
        compiler_params=_params("arbitrary"),
        name="matmul_cast_norm",
    )(h16, w32, norm_w.reshape(k, 1))


def _matmul_cast_res(x, w32, layer, res):
    m, k = x.shape
    n = w32.shape[2]
    tn = _cast_tile(k, n, 0)
    tile = pl.BlockSpec((m, tn), lambda j: (0, j))
    return pl.pallas_call(
        _mm_cast_res_kernel,
        grid=(n // tn,),
        in_specs=[pl.BlockSpec((m, k), lambda j: (0, 0)),
                  pl.BlockSpec((None, k, tn), lambda j: (layer, 0, j)),
                  tile],
        out_specs=[tile, tile, pl.BlockSpec((k, tn), lambda j: (0, j))],
        out_shape=[jax.ShapeDtypeStruct((m, n), f32), jax.ShapeDtypeStruct((m, n), bf16),
                   jax.ShapeDtypeStruct((k, n), bf16)],
        compiler_params=_params("arbitrary"),
        name="matmul_cast_res",
    )(x, w32, res)


def _gated_conv_proj_kernel(x_ref, wg_ref, wb_ref, wc_ref, wv_ref, cw_ref, hist_ref, o_ref, ho_ref,
                            halo, r_ref, *, tiles_per_stream):
    i = pl.program_id(0)
    j = pl.program_id(1)

    @pl.when(i % tiles_per_stream == 0)
    def _():
        halo[j] = hist_ref[0]

    @pl.when(j == 0)
    def _():
        r_ref[...] = _row_scale(x_ref[...])

    x = x_ref[...]
    r = r_ref[...]
    cur = (jnp.dot(x, wc_ref[...], preferred_element_type=f32)
           * jnp.dot(x, wv_ref[...], preferred_element_type=f32)) * (r * r)
    cv = _causal_conv(cur, halo[j], cw_ref[...])
    gate = jnp.dot(x, wg_ref[...], preferred_element_type=f32) * r
    o_ref[...] = ((gate * _sigmoid(gate)) * (jnp.dot(x, wb_ref[...], preferred_element_type=f32) * r)
                  * cv).astype(o_ref.dtype)
    tail = cur[cur.shape[0] - HALO:]
    halo[j] = tail
    ho_ref[0] = tail


def _proj_rows(stream_len):
    return _pick(stream_len, (1024, 512, 256, 128))


def _gated_conv_proj(x, w, conv_w, hist, bn):
    m, k = x.shape
    width = w.shape[1] // 4
    tm, tc = _proj_rows(m // bn), 256
    nj = width // tc
    tiles_per_stream = m // bn // tm
    taps = conv_w.shape[0]
    kern = functools.partial(_gated_conv_proj_kernel, tiles_per_stream=tiles_per_stream)
    wspec = lambda part: pl.BlockSpec((k, tc), lambda i, j: (0, part * nj + j))
    out, tails = pl.pallas_call(
        kern,
        grid=(m // tm, nj),
        in_specs=[pl.BlockSpec((tm, k), lambda i, j: (i, 0)),
                  wspec(0), wspec(1), wspec(2), wspec(3),
                  pl.BlockSpec((taps, tc), lambda i, j: (0, j)),
                  pl.BlockSpec((1, HALO, tc), lambda i, j: (0, 0, j))],
        out_specs=[pl.BlockSpec((tm, tc), lambda i, j: (i, j)),
                   pl.BlockSpec((1, HALO, tc), lambda i, j: (i, 0, j))],
        out_shape=[jax.ShapeDtypeStruct((m, width), bf16),
                   jax.ShapeDtypeStruct((m // tm, HALO, width), f32)],
        scratch_shapes=[pltpu.VMEM((nj, HALO, tc), f32), pltpu.VMEM((tm, 1), f32)],
        compiler_params=_params("arbitrary", "arbitrary"),
        name="gated_conv_proj",
    )(x, w, w, w, w, conv_w, hist)
    return out, tails[tiles_per_stream - 1::tiles_per_stream]


def _shift_matrix(taps, q):
    t = jnp.arange(q)[:, None]
    s = jnp.arange(q)[None, :]
    return jnp.concatenate([(s == t - (taps - 1 - k)) for k in range(taps)], axis=1).astype(bf16)


def _conv_silu(x16, hist, shift, w_ref, b_ref, lo, hi, taps, rows):
    w = w_ref[:, lo:hi]
    w16 = w.astype(bf16)
    scaled = jnp.concatenate([x16 * w16[k:k + 1, :] for k in range(taps)], axis=0)
    acc = jnp.dot(shift, scaled, preferred_element_type=f32)
    row = lax.broadcasted_iota(jnp.int32, hist.shape, 0)
    head = acc[0:HALO]
    for k in range(taps - 1):
        d = taps - 1 - k
        head = head + jnp.where(row < d, pltpu.roll(hist, d, 0), 0.0) * w[k:k + 1, :]
    acc = jnp.concatenate([head, acc[HALO:]], axis=0) + b_ref[:, lo:hi]
    tail = x16[rows - 2 * HALO:rows].astype(f32)[HALO:]
    return acc * _sigmoid(acc), tail


def _ssd_kernel(z_ref, x_ref, bc_ref, dt_ref, cxi_ref, cbci_ref, si_ref, shift_ref,
                cwx_ref, cwbc_ref, cbx_ref, cbbc_ref, dtb_ref, alog_ref, dsk_ref, nw_ref,
                y_ref, cxo_ref, cbco_ref, so_ref,
                hx, hbc, st, *, rows, n_chunks, taps, first_stream_empty):
    q = CHUNK
    d_inner = x_ref.shape[2]
    gw = d_inner // N_GROUPS
    heads_per_group = gw // HEAD_DIM
    c = pl.program_id(1)

    @pl.when(c == 0)
    def _():
        hx[...] = cxi_ref[0]
        hbc[...] = cbci_ref[0]
        for blk in range(d_inner // LANES):
            st[:, blk * LANES:(blk + 1) * LANES] = si_ref[0, blk * LANES:(blk + 1) * LANES, :].T

    if first_stream_empty:
        @pl.when((c == 0) & (pl.program_id(0) == 0))
        def _():
            st[...] = jnp.zeros_like(st)

    def pad_rows(v):
        if rows == q:
            return v
        return jnp.concatenate([v, jnp.zeros((q - rows, v.shape[1]), v.dtype)], axis=0)

    row_i = lax.broadcasted_iota(jnp.int32, (q, q), 0)
    col_i = lax.broadcasted_iota(jnp.int32, (q, q), 1)
    causal = row_i >= col_i
    v = pad_rows(dt_ref[0]) + dtb_ref[...]
    dt = jnp.maximum(v, 0.0) + jnp.log1p(jnp.exp(-jnp.abs(v)))
    if rows < q:
        dt = jnp.where(lax.broadcasted_iota(jnp.int32, dt.shape, 0) < rows, dt, 0.0)
    a = dt * (-jnp.exp(alog_ref[...]))
    acs = jnp.dot(causal.astype(f32), a, precision=lax.Precision.HIGHEST,
                  preferred_element_type=f32) * LOG2_E
    acs_t = acs.T
    src_t = acs_t - jnp.log2(dt.T)
    w_t = jnp.exp2(acs_t[:, q - 1:q] - src_t)

    n_state_cols = D_STATE * N_GROUPS
    bc16 = pad_rows(bc_ref[0])
    shift = shift_ref[...]
    b_all, b_tail = _conv_silu(bc16[:, 0:n_state_cols], hbc[:, 0:n_state_cols], shift,
                               cwbc_ref, cbbc_ref, 0, n_state_cols, taps, rows)
    c_all, c_tail = _conv_silu(bc16[:, n_state_cols:], hbc[:, n_state_cols:], shift,
                               cwbc_ref, cbbc_ref, n_state_cols, 2 * n_state_cols, taps, rows)
    hbc[:, 0:n_state_cols] = b_tail
    hbc[:, n_state_cols:] = c_tail
    lane = lax.broadcasted_iota(jnp.int32, (q, LANES), 1)
    first_head = lane < HEAD_DIM
    neg_inf = jnp.float32(-jnp.inf)

    for g in range(N_GROUPS):
        glo = g * gw
        b_g = b_all[:, g * D_STATE:(g + 1) * D_STATE]
        c_g = c_all[:, g * D_STATE:(g + 1) * D_STATE].astype(bf16)
        cb = lax.dot_general(c_g, b_g.astype(bf16), (((1,), (1,)), ((), ())),
                             preferred_element_type=f32)
        b_t = b_g.T
        s_g = st[:, glo:glo + gw]
        y_off = jnp.dot(c_g, s_g.astype(bf16), preferred_element_type=f32)
        x_g, x_tail = _conv_silu(pad_rows(x_ref[0, :, glo:glo + gw]), hx[:, glo:glo + gw], shift,
                                 cwx_ref, cbx_ref, glo, glo + gw, taps, rows)
        hx[:, glo:glo + gw] = x_tail

        y_parts = []
        for p in range(heads_per_group // 2):
            plo = p * LANES
            x_p = x_g[:, plo:plo + LANES]
            m_parts, bw_parts, e_parts = [], [], []
            for hh in range(2):
                h = g * heads_per_group + 2 * p + hh
                col_b = jnp.broadcast_to(acs[:, h:h + 1], (q, q))
                seg = col_b - src_t[h:h + 1, :]
                m_parts.append(cb * jnp.exp2(jnp.where(causal, seg, neg_inf)))
                bw_parts.append(b_t * w_t[h:h + 1, :])
                e_parts.append(jnp.exp2(col_b))
            lhs = jnp.concatenate([jnp.concatenate(m_parts, axis=1),
                                   jnp.concatenate(bw_parts, axis=1)], axis=0).astype(bf16)
            x_bd = jnp.concatenate([jnp.where(first_head, x_p, 0.0),
                                    jnp.where(first_head, 0.0, x_p)], axis=0).astype(bf16)
            res = jnp.dot(lhs, x_bd, preferred_element_type=f32)
            e_sel = jnp.where(first_head, e_parts[0], e_parts[1])
            y_parts.append(res[0:q] + y_off[:, plo:plo + LANES] * e_sel
                           + x_p * dsk_ref[:, glo + plo:glo + plo + LANES])
            st[:, glo + plo:glo + plo + LANES] = (s_g[:, plo:plo + LANES] * e_sel[q - 1:q, :]
                                                  + res[q:q + D_STATE])
        y_g = jnp.concatenate(y_parts, axis=1)
        z_g = pad_rows(z_ref[0, :, glo:glo + gw]).astype(f32)
        y_g = y_g * (z_g * _sigmoid(z_g))
        ms = jnp.mean(y_g * y_g, axis=-1, keepdims=True)
        y_n = y_g * lax.rsqrt(ms + EPS) * nw_ref[:, glo:glo + gw]
        y_ref[0, :, glo:glo + gw] = y_n[0:rows].astype(y_ref.dtype)

    @pl.when(c == n_chunks - 1)
    def _():
        cxo_ref[0] = hx[...]
        cbco_ref[0] = hbc[...]
        for blk in range(d_inner // LANES):
            so_ref[0, blk * LANES:(blk + 1) * LANES, :] = st[:, blk * LANES:(blk + 1) * LANES].T


def _ssd_core(proj, dt_raw, conv_x, conv_bc, state, cw_x, cw_bc, cb_x, cb_bc, dt_bias, a_log,
              d_skip_wide, norm_w, rows, shared_state):
    bn, length, _ = proj.shape
    d_inner = cw_x.shape[1]
    n_bc = cw_bc.shape[1]
    n_heads = dt_bias.shape[0]
    taps = cw_x.shape[0]
    n_chunks = length // rows
    first_stream_empty = not shared_state and state.shape[0] == bn - 1
    sidx = (lambda b, c: (0, 0, 0)) if shared_state else (lambda b, c: (b, 0, 0))
    ssm_idx = (lambda b, c: (jnp.maximum(b - 1, 0), 0, 0)) if first_stream_empty else sidx
    const = lambda b, c: (0, 0)
    vec = lambda n: pl.BlockSpec((1, n), const)
    kern = functools.partial(_ssd_kernel, rows=rows, n_chunks=n_chunks, taps=taps,
                             first_stream_empty=first_stream_empty)
    return pl.pallas_call(
        kern,
        grid=(bn, n_chunks),
        in_specs=[
            pl.BlockSpec((1, rows, d_inner), lambda b, c: (b, c, 0)),
            pl.BlockSpec((1, rows, d_inner), lambda b, c: (b, c, 1)),
            pl.BlockSpec((1, rows, n_bc), lambda b, c: (b, c, 2 * d_inner // n_bc)),
            pl.BlockSpec((1, rows, n_heads), lambda b, c: (b, c, 0)),
            pl.BlockSpec((1, HALO, d_inner), sidx),
            pl.BlockSpec((1, HALO, n_bc), sidx),
            pl.BlockSpec((1, d_inner, D_STATE), ssm_idx),
            pl.BlockSpec((CHUNK, taps * CHUNK), const),
            pl.BlockSpec((taps, d_inner), const),
            pl.BlockSpec((taps, n_bc), const),
            vec(d_inner), vec(n_bc), vec(n_heads), vec(n_heads), vec(d_inner), vec(d_inner),
        ],
        out_specs=[
            pl.BlockSpec((1, rows, d_inner), lambda b, c: (b, c, 0)),
            pl.BlockSpec((1, HALO, d_inner), lambda b, c: (b, 0, 0)),
            pl.BlockSpec((1, HALO, n_bc), lambda b, c: (b, 0, 0)),
            pl.BlockSpec((1, d_inner, D_STATE), lambda b, c: (b, 0, 0)),
        ],
        out_shape=[
            jax.ShapeDtypeStruct((bn, length, d_inner), bf16),
            jax.ShapeDtypeStruct((bn, HALO, d_inner), f32),
            jax.ShapeDtypeStruct((bn, HALO, n_bc), f32),
            jax.ShapeDtypeStruct((bn, d_inner, D_STATE), f32),
        ],
        scratch_shapes=[
            pltpu.VMEM((HALO, d_inner), f32),
            pltpu.VMEM((HALO, n_bc), f32),
            pltpu.VMEM((D_STATE, d_inner), f32),
        ],
        compiler_params=_params("arbitrary", "arbitrary"),
        name="ssd_core",
    )(proj, proj, proj, dt_raw, conv_x, conv_bc, state, _shift_matrix(taps, CHUNK), cw_x, cw_bc,
      cb_x.reshape(1, -1), cb_bc.reshape(1, -1), dt_bias.reshape(1, -1), a_log.reshape(1, -1),
      d_skip_wide.reshape(1, -1), norm_w.reshape(1, -1))


def _sc_kernel(g_ref, b_ref, c_ref, v_ref, ci_ref, w_ref, y_ref, co_ref, hist, *, rows, n_steps, taps):
    t = pl.program_id(2)

    @pl.when(t == 0)
    def _():
        hist[...] = ci_ref[0]

    cur = c_ref[0].astype(f32) * v_ref[0].astype(f32)
    cv = _causal_conv(cur, hist[...], w_ref[...])
    gate = g_ref[0].astype(f32)
    y_ref[0] = ((gate * _sigmoid(gate)) * b_ref[0].astype(f32) * cv).astype(y_ref.dtype)
    hist[...] = cur[rows - HALO:rows]

    @pl.when(t == n_steps - 1)
    def _():
        co_ref[0] = hist[...]


def _sc_core(proj, conv_in, conv_w, rows, shared_state):
    bn, length, four_w = proj.shape
    width = four_w // 4
    taps = conv_w.shape[0]
    tc = _pick(width, (1024, 512, 256, 128))
    nj = width // tc
    n_steps = length // rows
    sidx = (lambda b, j, t: (0, 0, j)) if shared_state else (lambda b, j, t: (b, 0, j))
    kern = functools.partial(_sc_kernel, rows=rows, n_steps=n_steps, taps=taps)
    col = lambda part: (lambda b, j, t: (b, t, part * nj + j))
    return pl.pallas_call(
        kern,
        grid=(bn, nj, n_steps),
        in_specs=[pl.BlockSpec((1, rows, tc), col(0)),
                  pl.BlockSpec((1, rows, tc), col(1)),
                  pl.BlockSpec((1, rows, tc), col(2)),
                  pl.BlockSpec((1, rows, tc), col(3)),
                  pl.BlockSpec((1, HALO, tc), sidx),
                  pl.BlockSpec((taps, tc), lambda b, j, t: (0, j))],
        out_specs=[pl.BlockSpec((1, rows, tc), lambda b, j, t: (b, t, j)),
                   pl.BlockSpec((1, HALO, tc), lambda b, j, t: (b, 0, j))],
        out_shape=[jax.ShapeDtypeStruct((bn, length, width), bf16),
                   jax.ShapeDtypeStruct((bn, HALO, width), f32)],
        scratch_shapes=[pltpu.VMEM((HALO, tc), f32)],
        compiler_params=_params("arbitrary", "arbitrary", "arbitrary"),
        name="sc_core",
    )(proj, proj, proj, proj, conv_in, conv_w)


def _pad_history(buf):
    return jnp.pad(buf, ((0, 0), (HALO - buf.shape[1], 0), (0, 0)))


def _cast_jobs(i, wts):
    j = i // 2
    norm_w = wts["ln_w"][i]
    if i % 2 == 0:
        n_all = wts["ssd_w_in"].shape[2]
        n_main = n_all - wts["ssd_dt_bias"].shape[1]
        jobs = [(wts["ssd_w_in"], j, norm_w, [(0, n_main), (n_main, n_all)]),
                (wts["ssd_w_out"], j, None, [(0, wts["ssd_w_out"].shape[2])])]
        keys = [("ssd_w_main", j), ("ssd_w_dt", j), ("ssd_w_out", j)]
    else:
        jobs = [(wts["sc_w_in"], j, norm_w, [(0, wts["sc_w_in"].shape[2])]),
                (wts["sc_w_out"], j, None, [(0, wts["sc_w_out"].shape[2])])]
        keys = [("sc_w_in", j), ("sc_w_out", j)]
    return jobs, keys


def _layer(i, grp, states, wts, w16, next_casts):
    h, h16, bn, rows, shared = grp["h"], grp["h16"], grp["bn"], grp["rows"], grp["shared"]
    m, d_model = h.shape
    length = m // bn
    j = i // 2
    norm_w = wts["ln_w"][i]

    def in_proj(name, src, col0, n, out_dtype):
        if (name, j) not in w16:
            out, w16[(name, j)] = _matmul_cast_norm(h16, wts[src], j, col0, n, norm_w, out_dtype)
            return out
        return _matmul_norm(h16, w16[(name, j)], out_dtype)

    def out_proj(y, name):
        if (name, j) not in w16:
            out, out16, w16[(name, j)] = _matmul_cast_res(y, wts[name], j, h)
        else:
            jobs, keys = next_casts
            out, out16, *cast = _matmul_res(y, w16[(name, j)], h, jobs)
            w16.update(zip(keys, cast))
        grp["h"], grp["h16"] = out, out16

    if i % 2 == 0:
        ssm, hist_x, hist_bc = states
        d_inner = wts["ssd_w_out"].shape[1]
        n_all = wts["ssd_w_in"].shape[2]
        n_main = n_all - wts["ssd_dt_bias"].shape[1]
        proj = in_proj("ssd_w_main", "ssd_w_in", 0, n_main, bf16)
        dt_raw = in_proj("ssd_w_dt", "ssd_w_in", n_main, n_all - n_main, f32)
        conv_w, conv_b = wts["ssd_conv_w"][j], wts["ssd_conv_b"][j]
        y, hx, hbc, s = _ssd_core(
            proj.reshape(bn, length, -1), dt_raw.reshape(bn, length, -1), hist_x, hist_bc, ssm,
            conv_w[:, :d_inner], conv_w[:, d_inner:], conv_b[:d_inner], conv_b[d_inner:],
            wts["ssd_dt_bias"][j], wts["ssd_a_log"][j], wts["ssd_d_wide"][j],
            wts["ssd_norm_w"][j], rows, shared)
        out_proj(y.reshape(m, d_inner), "ssd_w_out")
        return s, hx, hbc
    (hist,) = states
    if shared:
        in_proj_weights = w16[("sc_w_in", j)]
        y, hist = _gated_conv_proj(h16, in_proj_weights, wts["sc_conv_w"][j], hist, bn)
    else:
        proj = in_proj("sc_w_in", "sc_w_in", 0, wts["sc_w_in"].shape[2], bf16)
        y, hist = _sc_core(proj.reshape(bn, length, -1), hist, wts["sc_conv_w"][j], rows, shared)
    out_proj(y.reshape(m, -1), "sc_w_out")
    return (hist,)


def kernel(x_prompt, x_sample, state_ssm, state_ssd_conv, state_sc_conv, meta_tokens, ln_w, ssd_w_in,
           ssd_conv_w, ssd_conv_b, ssd_dt_bias, ssd_a_log, ssd_d_skip, ssd_norm_w, ssd_w_out,
           sc_w_in, sc_conv_w, sc_w_out, final_norm_w):
    batch, seq, d_model = x_prompt.shape
    dec_batch, dec_seq, _ = x_sample.shape
    n_ssd, _, n_heads, head_dim, d_state = state_ssm.shape
    n_sc = state_sc_conv.shape[0]
    d_inner = n_heads * head_dim
    assert head_dim == HEAD_DIM and d_state == D_STATE
    assert meta_tokens.shape[0] == dec_seq and dec_seq % (2 * SUBLANES) == 0 and dec_seq <= CHUNK
    assert seq % CHUNK == 0

    wts = {
        "ln_w": ln_w, "final_norm_w": final_norm_w,
        "ssd_w_in": ssd_w_in, "ssd_w_out": ssd_w_out, "sc_w_in": sc_w_in, "sc_w_out": sc_w_out,
        "ssd_conv_w": ssd_conv_w, "ssd_conv_b": ssd_conv_b, "ssd_dt_bias": ssd_dt_bias,
        "ssd_a_log": ssd_a_log, "ssd_norm_w": ssd_norm_w, "sc_conv_w": sc_conv_w,
        "ssd_d_wide": jnp.repeat(ssd_d_skip, head_dim, axis=1),
    }

    n_small = 1 + dec_batch
    h_small = jnp.concatenate([meta_tokens[None].astype(x_sample.dtype), x_sample], axis=0)
    zero_lead = lambda a: jnp.concatenate([jnp.zeros((1,) + a.shape[1:], a.dtype), a], axis=0)
    ssm_in, hx_in, hbc_in, sc_in = [], [], [], []
    for j in range(n_ssd):
        ssm_in.append(state_ssm[j].reshape(dec_batch, d_inner, d_state))
        hist = _pad_history(zero_lead(state_ssd_conv[j]))
        hx_in.append(hist[:, :, :d_inner])
        hbc_in.append(hist[:, :, d_inner:])
    for j in range(n_sc):
        sc_in.append(_pad_history(zero_lead(state_sc_conv[j])))

    def group(h, bn, rows, shared):
        h = h.reshape(-1, d_model)
        return dict(h=h, h16=h.astype(bf16), bn=bn, rows=rows, shared=shared)

    small = group(h_small, n_small, dec_seq, False)
    prompt = group(x_prompt, batch, CHUNK, True)
    w16 = {}
    ssm_s, hx_s, hbc_s, sc_s, ssm_p, hx_p, hbc_p, sc_p = ([] for _ in range(8))
    depth = ln_w.shape[0]
    for i in range(depth):
        j = i // 2
        ssd = i % 2 == 0
        new_s = _layer(i, small, (ssm_in[j], hx_in[j], hbc_in[j]) if ssd else (sc_in[j],),
                       wts, w16, ((), ()))
        casts = _cast_jobs(i + 1, wts) if i + 1 < depth else ((), ())
        new_p = _layer(i, prompt, tuple(a[0:1] for a in new_s), wts, w16, casts)
        for dst, val in zip((ssm_s, hx_s, hbc_s) if ssd else (sc_s,), new_s):
            dst.append(val)
        for dst, val in zip((ssm_p, hx_p, hbc_p) if ssd else (sc_p,), new_p):
            dst.append(val)
    y_small = _rmsnorm(small["h"], final_norm_w, f32)
    y_p = _rmsnorm(prompt["h"], final_norm_w, f32)

    def ssm_out(parts, lo):
        return jnp.stack([s[lo:].reshape(-1, n_heads, head_dim, d_state) for s in parts])

    def ssd_hist_out(px, pbc, lo, k):
        return jnp.stack([jnp.concatenate([a[lo:, HALO - k:], b[lo:, HALO - k:]], axis=-1)
                          for a, b in zip(px, pbc)])

    def sc_hist_out(parts, lo, k):
        return jnp.stack([a[lo:, HALO - k:] for a in parts])

    k_ssd = state_ssd_conv.shape[2]
    k_sc = state_sc_conv.shape[2]
    y_prompt = y_p.reshape(batch, seq, d_model)
    y_sample = y_small.reshape(n_small, dec_seq, d_model)[1:]
    return (y_prompt, y_sample,
            ssm_out(ssm_p, 0), ssd_hist_out(hx_p, hbc_p, 0, k_ssd), sc_hist_out(sc_p, 0, k_sc),
            ssm_out(ssm_s, 1), ssd_hist_out(hx_s, hbc_s, 1, k_ssd), sc_hist_out(sc_s, 1, k_sc))
```

```python
import functools

import jax
import jax.numpy as jnp
from jax import lax
from jax.experimental import pallas as pl
from jax.experimental.pallas import tpu as pltpu

EPS = 1e-5
HEAD_DIM = 64
D_STATE = 128
N_GROUPS = 8
LANES = 128
SUBLANES = 8
CHUNK = 128
HALO = SUBLANES
VMEM_LIMIT = 56 * 1024 * 1024
LOG2_E = 1.4426950408889634

f32 = jnp.float32
bf16 = jnp.bfloat16


def _sigmoid(x):
    return 1.0 / (1.0 + jnp.exp2(x * (-LOG2_E)))


def _causal_conv(cur, prev, w):
    taps = w.shape[0]
    row = lax.broadcasted_iota(jnp.int32, prev.shape, 0)
    acc = None
    for k in range(taps):
        d = taps - 1 - k
        part = cur
        if d > 0:
            rolled = pltpu.roll(cur, d, 0)
            head = jnp.where(row < d, pltpu.roll(prev, d, 0), rolled[0:HALO])
            part = jnp.concatenate([head, rolled[HALO:]], axis=0)
        term = part * w[k:k + 1, :]
        acc = term if acc is None else acc + term
    return acc


def _params(*sem):
    return pltpu.CompilerParams(dimension_semantics=sem, vmem_limit_bytes=VMEM_LIMIT)


def _rmsnorm_kernel(x_ref, w_ref, o_ref):
    x = x_ref[...]
    ms = jnp.mean(x * x, axis=-1, keepdims=True)
    o_ref[...] = (x * lax.rsqrt(ms + EPS) * w_ref[...]).astype(o_ref.dtype)


def _rmsnorm(x, w, out_dtype):
    m, d = x.shape
    tm = 256 if m % 256 == 0 else m
    return pl.pallas_call(
        _rmsnorm_kernel,
        grid=(m // tm,),
        in_specs=[pl.BlockSpec((tm, d), lambda i: (i, 0)),
                  pl.BlockSpec((1, d), lambda i: (0, 0))],
        out_specs=pl.BlockSpec((tm, d), lambda i: (i, 0)),
        out_shape=jax.ShapeDtypeStruct((m, d), out_dtype),
        compiler_params=_params("arbitrary"),
        name="rmsnorm",
    )(x, w.reshape(1, d))


def _row_scale(h16):
    hf = h16.astype(f32)
    return lax.rsqrt(jnp.mean(hf * hf, axis=-1, keepdims=True) + EPS)


def _mm_norm_kernel(h_ref, w_ref, *refs, extra, side):
    n_x = 1 if extra else 0
    n_in = _n_cast_inputs(side)
    we_ref = refs[0] if extra else None
    o_ref = refs[n_x + n_in]
    oe_ref = refs[n_x + n_in + 1] if extra else None
    r_ref = refs[-1]
    _run_casts(side, refs[n_x:n_x + n_in], refs[2 * n_x + n_in + 1:-1])

    @pl.when(pl.program_id(1) == 0)
    def _():
        r = _row_scale(h_ref[...])
        r_ref[...] = r
        if extra:
            oe_ref[...] = jnp.dot(h_ref[...], we_ref[...], preferred_element_type=f32) * r

    acc = jnp.dot(h_ref[...], w_ref[...], preferred_element_type=f32)
    o_ref[...] = (acc * r_ref[...]).astype(o_ref.dtype)


def _cast_specs(casts, n_col_steps, n_steps):
    in_specs, args, out_specs, out_shape, layout = [], [], [], [], []
    for w32, layer, norm_w, splits in casts:
        rows, cols = w32.shape[1:]
        pack = 2 * SUBLANES
        rb = next(r for r in range(pack, rows + 1, pack) if rows % r == 0 and rows // r <= n_steps)
        n_blocks = rows // rb
        blk = lambda i, j, n_blocks=n_blocks: jnp.minimum(i * n_col_steps + j, n_blocks - 1)
        in_specs.append(pl.BlockSpec((None, rb, cols),
                                     lambda i, j, layer=layer, blk=blk: (layer, blk(i, j), 0)))
        args.append(w32)
        if norm_w is not None:
            in_specs.append(pl.BlockSpec((rb, 1), lambda i, j, blk=blk: (blk(i, j), 0)))
            args.append(norm_w.reshape(rows, 1))
        for lo, hi in splits:
            out_specs.append(pl.BlockSpec((rb, hi - lo), lambda i, j, blk=blk: (blk(i, j), 0)))
            out_shape.append(jax.ShapeDtypeStruct((rows, hi - lo), bf16))
        layout.append((norm_w is not None, tuple(splits)))
    return in_specs, args, out_specs, out_shape, tuple(layout)


def _n_cast_inputs(layout):
    return sum(2 if has_norm else 1 for has_norm, _ in layout)


def _run_casts(layout, in_refs, out_refs):
    src = dst = 0
    for has_norm, splits in layout:
        block = in_refs[src][...]
        if has_norm:
            block = block * in_refs[src + 1][...]
        src += 2 if has_norm else 1
        for lo, hi in splits:
            out_refs[dst][...] = block[:, lo:hi].astype(bf16)
            dst += 1


def _mm_res_kernel(x_ref, w_ref, r_ref, *refs, side):
    n_in = _n_cast_inputs(side)
    o_ref, o16_ref = refs[n_in], refs[n_in + 1]
    out = r_ref[...] + jnp.dot(x_ref[...], w_ref[...], preferred_element_type=f32)
    o_ref[...] = out
    o16_ref[...] = out.astype(bf16)
    _run_casts(side, refs[:n_in], refs[n_in + 2:])


def _pick(n, candidates):
    for c in candidates:
        if n % c == 0:
            return c
    return n


def _mm_tiles(m, k, n, with_res):
    tm = _pick(m, (1024, 512, 256))
    budget = 48 * 1024 * 1024
    out_bytes = 4 + 4 + 2 if with_res else 4
    for tn in (1024, 512, 256, 128):
        if n % tn:
            continue
        need = 2 * (tm * k * 2 + k * tn * 2 + tm * tn * out_bytes)
        if need <= budget:
            return tm, tn
    return tm, _pick(n, (128,))


def _matmul_norm(h16, w, out_dtype, w_extra=None, casts=()):
    m, k = h16.shape
    n = w.shape[1]
    tm, tn = _mm_tiles(m, k, n, False)
    nj = n // tn
    in_specs = [pl.BlockSpec((tm, k), lambda i, j: (i, 0)),
                pl.BlockSpec((k, tn), lambda i, j: (0, j))]
    out_specs = [pl.BlockSpec((tm, tn), lambda i, j: (i, j))]
    out_shape = [jax.ShapeDtypeStruct((m, n), out_dtype)]
    args = [h16, w]
    if w_extra is not None:
        ne = w_extra.shape[1]
        in_specs.append(pl.BlockSpec((k, ne), lambda i, j: (0, 0)))
        out_specs.append(pl.BlockSpec((tm, ne), lambda i, j: (i, 0)))
        out_shape.append(jax.ShapeDtypeStruct((m, ne), f32))
        args.append(w_extra)
    c_in, c_args, c_out, c_shape, side = _cast_specs(casts, nj, (m // tm) * nj)
    res = pl.pallas_call(
        functools.partial(_mm_norm_kernel, extra=w_extra is not None, side=side),
        grid=(m // tm, nj),
        in_specs=in_specs + c_in,
        out_specs=out_specs + c_out,
        out_shape=out_shape + c_shape,
        scratch_shapes=[pltpu.VMEM((tm, 1), f32)],
        compiler_params=_params("arbitrary", "arbitrary"),
        name="matmul_norm",
    )(*args, *c_args)
    return res if len(res) > 1 else res[0]


def _matmul_res(x, w, res, casts=()):
    m, k = x.shape
    n = w.shape[1]
    tm, tn = _mm_tiles(m, k, n, True)
    nj = n // tn
    tile = pl.BlockSpec((tm, tn), lambda i, j: (i, j))
    c_in, c_args, c_out, c_shape, side = _cast_specs(casts, nj, (m // tm) * nj)
    return pl.pallas_call(
        functools.partial(_mm_res_kernel, side=side),
        grid=(m // tm, nj),
        in_specs=[pl.BlockSpec((tm, k), lambda i, j: (i, 0)),
                  pl.BlockSpec((k, tn), lambda i, j: (0, j)),
                  tile] + c_in,
        out_specs=[tile, tile] + c_out,
        out_shape=[jax.ShapeDtypeStruct((m, n), f32), jax.ShapeDtypeStruct((m, n), bf16)] + c_shape,
        compiler_params=_params("arbitrary", "arbitrary"),
        name="matmul_res",
    )(x, w, res, *c_args)


def _mm_cast_norm_kernel(h_ref, w_ref, nw_ref, o_ref, wb_ref):
    wb = (w_ref[...] * nw_ref[...]).astype(bf16)
    wb_ref[...] = wb
    acc = jnp.dot(h_ref[...], wb, preferred_element_type=f32)
    o_ref[...] = (acc * _row_scale(h_ref[...])).astype(o_ref.dtype)


def _mm_cast_res_kernel(x_ref, w_ref, r_ref, o_ref, o16_ref, wb_ref):
    wb = w_ref[...].astype(bf16)
    wb_ref[...] = wb
    out = r_ref[...] + jnp.dot(x_ref[...], wb, preferred_element_type=f32)
    o_ref[...] = out
    o16_ref[...] = out.astype(bf16)


def _cast_tile(k, n, col0):
    tile_bytes = 8 * 1024 * 1024
    return next(t for t in (512, 256, 128) if n % t == 0 and col0 % t == 0 and k * t * 4 <= tile_bytes)


def _matmul_cast_norm(h16, w32, layer, col0, n, norm_w, out_dtype):
    m, k = h16.shape
    tn = _cast_tile(k, n, col0)
    j0 = col0 // tn
    return pl.pallas_call(
        _mm_cast_norm_kernel,
        grid=(n // tn,),
        in_specs=[pl.BlockSpec((m, k), lambda j: (0, 0)),
                  pl.BlockSpec((None, k, tn), lambda j: (layer, 0, j0 + j)),
                  pl.BlockSpec((k, 1), lambda j: (0, 0))],
        out_specs=[pl.BlockSpec((m, tn), lambda j: (0, j)),
                   pl.BlockSpec((k, tn), lambda j: (0, j))],
        out_shape=[jax.ShapeDtypeStruct((m, n), out_dtype),
                   jax.ShapeDtypeStruct((k, n), bf16)],
        compiler_params=_params("arbitrary"),
        name="matmul_cast_norm",
    )(h16, w32, norm_w.reshape(k, 1))


def _matmul_cast_res(x, w32, layer, res):
    m, k = x.shape
    n = w32.shape[2]
    tn = _cast_tile(k, n, 0)
    tile = pl.BlockSpec((m, tn), lambda j: (0, j))
    return pl.pallas_call(
        _mm_cast_res_kernel,
        grid=(n // tn,),
        in_specs=[pl.BlockSpec((m, k), lambda j: (0, 0)),
                  pl.BlockSpec((None, k, tn), lambda j: (layer, 0, j)),
                  tile],
        out_specs=[tile, tile, pl.BlockSpec((k, tn), lambda j: (0, j))],
        out_shape=[jax.ShapeDtypeStruct((m, n), f32), jax.ShapeDtypeStruct((m, n), bf16),
                   jax.ShapeDtypeStruct((k, n), bf16)],
        compiler_params=_params("arbitrary"),
        name="matmul_cast_res",
    )(x, w32, res)


def _gated_conv_proj_kernel(x_ref, wg_ref, wb_ref, wc_ref, wv_ref, cw_ref, hist_ref, *refs,
                            tiles_per_stream, side):
    n_in = _n_cast_inputs(side)
    o_ref, ho_ref = refs[n_in], refs[n_in + 1]
    halo, r_ref = refs[-2], refs[-1]
    _run_casts(side, refs[:n_in], refs[n_in + 2:-2])
    i = pl.program_id(0)
    j = pl.program_id(1)

    @pl.when(i % tiles_per_stream == 0)
    def _():
        halo[j] = hist_ref[0]

    @pl.when(j == 0)
    def _():
        r_ref[...] = _row_scale(x_ref[...])

    x = x_ref[...]
    r = r_ref[...]
    cur = (jnp.dot(x, wc_ref[...], preferred_element_type=f32)
           * jnp.dot(x, wv_ref[...], preferred_element_type=f32)) * (r * r)
    cv = _causal_conv(cur, halo[j], cw_ref[...])
    gate = jnp.dot(x, wg_ref[...], preferred_element_type=f32) * r
    o_ref[...] = ((gate * _sigmoid(gate)) * (jnp.dot(x, wb_ref[...], preferred_element_type=f32) * r)
                  * cv).astype(o_ref.dtype)
    tail = cur[cur.shape[0] - HALO:]
    halo[j] = tail
    ho_ref[0] = tail


def _proj_rows(stream_len):
    return _pick(stream_len, (1024, 512, 256, 128))


def _gated_conv_proj(x, w, conv_w, hist, bn, casts=()):
    m, k = x.shape
    width = w.shape[1] // 4
    tm, tc = _proj_rows(m // bn), 256
    nj = width // tc
    tiles_per_stream = m // bn // tm
    taps = conv_w.shape[0]
    c_in, c_args, c_out, c_shape, side = _cast_specs(casts, nj, (m // tm) * nj)
    kern = functools.partial(_gated_conv_proj_kernel, tiles_per_stream=tiles_per_stream, side=side)
    wspec = lambda part: pl.BlockSpec((k, tc), lambda i, j: (0, part * nj + j))
    out, tails, *cast = pl.pallas_call(
        kern,
        grid=(m // tm, nj),
        in_specs=[pl.BlockSpec((tm, k), lambda i, j: (i, 0)),
                  wspec(0), wspec(1), wspec(2), wspec(3),
                  pl.BlockSpec((taps, tc), lambda i, j: (0, j)),
                  pl.BlockSpec((1, HALO, tc), lambda i, j: (0, 0, j))] + c_in,
        out_specs=[pl.BlockSpec((tm, tc), lambda i, j: (i, j)),
                   pl.BlockSpec((1, HALO, tc), lambda i, j: (i, 0, j))] + c_out,
        out_shape=[jax.ShapeDtypeStruct((m, width), bf16),
                   jax.ShapeDtypeStruct((m // tm, HALO, width), f32)] + c_shape,
        scratch_shapes=[pltpu.VMEM((nj, HALO, tc), f32), pltpu.VMEM((tm, 1), f32)],
        compiler_params=_params("arbitrary", "arbitrary"),
        name="gated_conv_proj",
    )(x, w, w, w, w, conv_w, hist, *c_args)
    return (out, tails[tiles_per_stream - 1::tiles_per_stream], *cast)


def _shift_matrix(taps, q):
    t = jnp.arange(q)[:, None]
    s = jnp.arange(q)[None, :]
    return jnp.concatenate([(s == t - (taps - 1 - k)) for k in range(taps)], axis=1).astype(bf16)


def _conv_silu(x16, hist, shift, w_ref, b_ref, lo, hi, taps, rows):
    w = w_ref[:, lo:hi]
    w16 = w.astype(bf16)
    scaled = jnp.concatenate([x16 * w16[k:k + 1, :] for k in range(taps)], axis=0)
    acc = jnp.dot(shift, scaled, preferred_element_type=f32)
    row = lax.broadcasted_iota(jnp.int32, hist.shape, 0)
    head = acc[0:HALO]
    for k in range(taps - 1):
        d = taps - 1 - k
        head = head + jnp.where(row < d, pltpu.roll(hist, d, 0), 0.0) * w[k:k + 1, :]
    acc = jnp.concatenate([head, acc[HALO:]], axis=0) + b_ref[:, lo:hi]
    tail = x16[rows - 2 * HALO:rows].astype(f32)[HALO:]
    return acc * _sigmoid(acc), tail


def _ssd_kernel(z_ref, x_ref, bc_ref, dt_ref, cxi_ref, cbci_ref, si_ref, shift_ref,
                cwx_ref, cwbc_ref, cbx_ref, cbbc_ref, dtb_ref, alog_ref, dsk_ref, nw_ref,
                *refs, rows, n_chunks, taps, first_stream_empty, aliased):
    if aliased:
        refs = refs[1:]
    y_ref, cxo_ref, cbco_ref = refs[:3]
    so_refs = refs[3:-3]
    hx, hbc, st = refs[-3:]
    q = CHUNK
    d_inner = x_ref.shape[2]
    gw = d_inner // N_GROUPS
    heads_per_group = gw // HEAD_DIM
    c = pl.program_id(1)

    @pl.when(c == 0)
    def _():
        hx[...] = cxi_ref[0]
        hbc[...] = cbci_ref[0]
        for blk in range(d_inner // LANES):
            st[:, blk * LANES:(blk + 1) * LANES] = si_ref[0, blk * LANES:(blk + 1) * LANES, :].T

    if first_stream_empty:
        @pl.when((c == 0) & (pl.program_id(0) == 0))
        def _():
            st[...] = jnp.zeros_like(st)

    def pad_rows(v):
        if rows == q:
            return v
        return jnp.concatenate([v, jnp.zeros((q - rows, v.shape[1]), v.dtype)], axis=0)

    row_i = lax.broadcasted_iota(jnp.int32, (q, q), 0)
    col_i = lax.broadcasted_iota(jnp.int32, (q, q), 1)
    causal = row_i >= col_i
    v = pad_rows(dt_ref[0]) + dtb_ref[...]
    dt = jnp.maximum(v, 0.0) + jnp.log1p(jnp.exp(-jnp.abs(v)))
    if rows < q:
        dt = jnp.where(lax.broadcasted_iota(jnp.int32, dt.shape, 0) < rows, dt, 0.0)
    a = dt * (-jnp.exp(alog_ref[...]))
    acs = jnp.dot(causal.astype(f32), a, precision=lax.Precision.HIGHEST,
                  preferred_element_type=f32) * LOG2_E
    acs_t = acs.T
    src_t = acs_t - jnp.log2(dt.T)
    w_t = jnp.exp2(acs_t[:, q - 1:q] - src_t)

    n_state_cols = D_STATE * N_GROUPS
    bc16 = pad_rows(bc_ref[0])
    shift = shift_ref[...]
    b_all, b_tail = _conv_silu(bc16[:, 0:n_state_cols], hbc[:, 0:n_state_cols], shift,
                               cwbc_ref, cbbc_ref, 0, n_state_cols, taps, rows)
    c_all, c_tail = _conv_silu(bc16[:, n_state_cols:], hbc[:, n_state_cols:], shift,
                               cwbc_ref, cbbc_ref, n_state_cols, 2 * n_state_cols, taps, rows)
    hbc[:, 0:n_state_cols] = b_tail
    hbc[:, n_state_cols:] = c_tail
    lane = lax.broadcasted_iota(jnp.int32, (q, LANES), 1)
    first_head = lane < HEAD_DIM
    neg_inf = jnp.float32(-jnp.inf)

    for g in range(N_GROUPS):
        glo = g * gw
        b_g = b_all[:, g * D_STATE:(g + 1) * D_STATE]
        c_g = c_all[:, g * D_STATE:(g + 1) * D_STATE].astype(bf16)
        cb = lax.dot_general(c_g, b_g.astype(bf16), (((1,), (1,)), ((), ())),
                             preferred_element_type=f32)
        b_t = b_g.T
        s_g = st[:, glo:glo + gw]
        y_off = jnp.dot(c_g, s_g.astype(bf16), preferred_element_type=f32)
        x_g, x_tail = _conv_silu(pad_rows(x_ref[0, :, glo:glo + gw]), hx[:, glo:glo + gw], shift,
                                 cwx_ref, cbx_ref, glo, glo + gw, taps, rows)
        hx[:, glo:glo + gw] = x_tail

        y_parts = []
        for p in range(heads_per_group // 2):
            plo = p * LANES
            x_p = x_g[:, plo:plo + LANES]
            m_parts, bw_parts, e_parts = [], [], []
            for hh in range(2):
                h = g * heads_per_group + 2 * p + hh
                col_b = jnp.broadcast_to(acs[:, h:h + 1], (q, q))
                seg = col_b - src_t[h:h + 1, :]
                m_parts.append(cb * jnp.exp2(jnp.where(causal, seg, neg_inf)))
                bw_parts.append(b_t * w_t[h:h + 1, :])
                e_parts.append(jnp.exp2(col_b))
            lhs = jnp.concatenate([jnp.concatenate(m_parts, axis=1),
                                   jnp.concatenate(bw_parts, axis=1)], axis=0).astype(bf16)
            x_bd = jnp.concatenate([jnp.where(first_head, x_p, 0.0),
                                    jnp.where(first_head, 0.0, x_p)], axis=0).astype(bf16)
            res = jnp.dot(lhs, x_bd, preferred_element_type=f32)
            e_sel = jnp.where(first_head, e_parts[0], e_parts[1])
            y_parts.append(res[0:q] + y_off[:, plo:plo + LANES] * e_sel
                           + x_p * dsk_ref[:, glo + plo:glo + plo + LANES])
            st[:, glo + plo:glo + plo + LANES] = (s_g[:, plo:plo + LANES] * e_sel[q - 1:q, :]
                                                  + res[q:q + D_STATE])
        y_g = jnp.concatenate(y_parts, axis=1)
        z_g = pad_rows(z_ref[0, :, glo:glo + gw]).astype(f32)
        y_g = y_g * (z_g * _sigmoid(z_g))
        ms = jnp.mean(y_g * y_g, axis=-1, keepdims=True)
        y_n = y_g * lax.rsqrt(ms + EPS) * nw_ref[:, glo:glo + gw]
        y_ref[0, :, glo:glo + gw] = y_n[0:rows].astype(y_ref.dtype)

    last = c == n_chunks - 1

    def write_state(so_ref):
        for blk in range(d_inner // LANES):
            so_ref[0, blk * LANES:(blk + 1) * LANES, :] = st[:, blk * LANES:(blk + 1) * LANES].T

    @pl.when(last)
    def _():
        cxo_ref[0] = hx[...]
        cbco_ref[0] = hbc[...]

    if first_stream_empty:
        @pl.when(last & (pl.program_id(0) == 0))
        def _():
            write_state(so_refs[0])

        @pl.when(last & (pl.program_id(0) > 0))
        def _():
            write_state(so_refs[1])
    else:
        @pl.when(last)
        def _():
            write_state(so_refs[0])


def _ssd_core(proj, dt_raw, conv_x, conv_bc, state, cw_x, cw_bc, cb_x, cb_bc, dt_bias, a_log,
              d_skip_wide, norm_w, rows, shared_state, state_base, state_out):
    bn, length, _ = proj.shape
    d_inner = cw_x.shape[1]
    n_bc = cw_bc.shape[1]
    n_heads = dt_bias.shape[0]
    taps = cw_x.shape[0]
    n_chunks = length // rows
    first_stream_empty = not shared_state
    earlier, n_layers, layer = state_out
    sidx = (lambda b, c: (0, 0, 0)) if shared_state else (lambda b, c: (b, 0, 0))
    rest = lambda b: jnp.maximum(b - 1, 0)
    ssm_idx = sidx if shared_state else (lambda b, c: (state_base + rest(b), 0, 0))
    const = lambda b, c: (0, 0)
    vec = lambda n: pl.BlockSpec((1, n), const)
    slab = (None, 1, d_inner, D_STATE)
    if first_stream_empty:
        state_specs = [pl.BlockSpec((1, d_inner, D_STATE), lambda b, c: (0, 0, 0)),
                       pl.BlockSpec(slab, lambda b, c: (layer, rest(b), 0, 0))]
        state_shapes = [jax.ShapeDtypeStruct((1, d_inner, D_STATE), f32),
                        jax.ShapeDtypeStruct((n_layers, bn - 1, d_inner, D_STATE), f32)]
    else:
        state_specs = [pl.BlockSpec(slab, lambda b, c: (layer, b, 0, 0))]
        state_shapes = [jax.ShapeDtypeStruct((n_layers, bn, d_inner, D_STATE), f32)]
    kern = functools.partial(_ssd_kernel, rows=rows, n_chunks=n_chunks, taps=taps,
                             first_stream_empty=first_stream_empty, aliased=earlier is not None)
    n_inputs = 16
    extra_in, extra_args, aliases = [], [], {}
    if earlier is not None:
        extra_in, extra_args = [pl.BlockSpec(memory_space=pl.ANY)], [earlier]
        aliases = {n_inputs: 3 + len(state_specs) - 1}
    return pl.pallas_call(
        kern,
        grid=(bn, n_chunks),
        input_output_aliases=aliases,
        in_specs=[
            pl.BlockSpec((1, rows, d_inner), lambda b, c: (b, c, 0)),
            pl.BlockSpec((1, rows, d_inner), lambda b, c: (b, c, 1)),
            pl.BlockSpec((1, rows, n_bc), lambda b, c: (b, c, 2 * d_inner // n_bc)),
            pl.BlockSpec((1, rows, n_heads), lambda b, c: (b, c, 0)),
            pl.BlockSpec((1, HALO, d_inner), sidx),
            pl.BlockSpec((1, HALO, n_bc), sidx),
            pl.BlockSpec((1, d_inner, D_STATE), ssm_idx),
            pl.BlockSpec((CHUNK, taps * CHUNK), const),
            pl.BlockSpec((taps, d_inner), const),
            pl.BlockSpec((taps, n_bc), const),
            vec(d_inner), vec(n_bc), vec(n_heads), vec(n_heads), vec(d_inner), vec(d_inner),
        ] + extra_in,
        out_specs=[
            pl.BlockSpec((1, rows, d_inner), lambda b, c: (b, c, 0)),
            pl.BlockSpec((1, HALO, d_inner), lambda b, c: (b, 0, 0)),
            pl.BlockSpec((1, HALO, n_bc), lambda b, c: (b, 0, 0)),
        ] + state_specs,
        out_shape=[
            jax.ShapeDtypeStruct((bn, length, d_inner), bf16),
            jax.ShapeDtypeStruct((bn, HALO, d_inner), f32),
            jax.ShapeDtypeStruct((bn, HALO, n_bc), f32),
        ] + state_shapes,
        scratch_shapes=[
            pltpu.VMEM((HALO, d_inner), f32),
            pltpu.VMEM((HALO, n_bc), f32),
            pltpu.VMEM((D_STATE, d_inner), f32),
        ],
        compiler_params=_params("arbitrary", "arbitrary"),
        name="ssd_core",
    )(proj, proj, proj, dt_raw, conv_x, conv_bc, state, _shift_matrix(taps, CHUNK), cw_x, cw_bc,
      cb_x.reshape(1, -1), cb_bc.reshape(1, -1), dt_bias.reshape(1, -1), a_log.reshape(1, -1),
      d_skip_wide.reshape(1, -1), norm_w.reshape(1, -1), *extra_args)


def _sc_kernel(g_ref, b_ref, c_ref, v_ref, ci_ref, w_ref, y_ref, co_ref, hist, *, rows, n_steps, taps):
    t = pl.program_id(2)

    @pl.when(t == 0)
    def _():
        hist[...] = ci_ref[0]

    cur = c_ref[0].astype(f32) * v_ref[0].astype(f32)
    cv = _causal_conv(cur, hist[...], w_ref[...])
    gate = g_ref[0].astype(f32)
    y_ref[0] = ((gate * _sigmoid(gate)) * b_ref[0].astype(f32) * cv).astype(y_ref.dtype)
    hist[...] = cur[rows - HALO:rows]

    @pl.when(t == n_steps - 1)
    def _():
        co_ref[0] = hist[...]


def _sc_core(proj, conv_in, conv_w, rows, shared_state):
    bn, length, four_w = proj.shape
    width = four_w // 4
    taps = conv_w.shape[0]
    tc = _pick(width, (1024, 512, 256, 128))
    nj = width // tc
    n_steps = length // rows
    sidx = (lambda b, j, t: (0, 0, j)) if shared_state else (lambda b, j, t: (b, 0, j))
    kern = functools.partial(_sc_kernel, rows=rows, n_steps=n_steps, taps=taps)
    col = lambda part: (lambda b, j, t: (b, t, part * nj + j))
    return pl.pallas_call(
        kern,
        grid=(bn, nj, n_steps),
        in_specs=[pl.BlockSpec((1, rows, tc), col(0)),
                  pl.BlockSpec((1, rows, tc), col(1)),
                  pl.BlockSpec((1, rows, tc), col(2)),
                  pl.BlockSpec((1, rows, tc), col(3)),
                  pl.BlockSpec((1, HALO, tc), sidx),
                  pl.BlockSpec((taps, tc), lambda b, j, t: (0, j))],
        out_specs=[pl.BlockSpec((1, rows, tc), lambda b, j, t: (b, t, j)),
                   pl.BlockSpec((1, HALO, tc), lambda b, j, t: (b, 0, j))],
        out_shape=[jax.ShapeDtypeStruct((bn, length, width), bf16),
                   jax.ShapeDtypeStruct((bn, HALO, width), f32)],
        scratch_shapes=[pltpu.VMEM((HALO, tc), f32)],
        compiler_params=_params("arbitrary", "arbitrary", "arbitrary"),
        name="sc_core",
    )(proj, proj, proj, proj, conv_in, conv_w)


def _pad_history(buf):
    return jnp.pad(buf, ((0, 0), (HALO - buf.shape[1], 0), (0, 0)))


def _cast_jobs(i, wts):
    j = i // 2
    norm_w = wts["ln_w"][i]
    if i % 2 == 0:
        n_all = wts["ssd_w_in"].shape[2]
        n_main = n_all - wts["ssd_dt_bias"].shape[1]
        jobs = [(wts["ssd_w_in"], j, norm_w, [(0, n_main), (n_main, n_all)]),
                (wts["ssd_w_out"], j, None, [(0, wts["ssd_w_out"].shape[2])])]
        keys = [("ssd_w_main", j), ("ssd_w_dt", j), ("ssd_w_out", j)]
    else:
        jobs = [(wts["sc_w_in"], j, norm_w, [(0, wts["sc_w_in"].shape[2])]),
                (wts["sc_w_out"], j, None, [(0, wts["sc_w_out"].shape[2])])]
        keys = [("sc_w_in", j), ("sc_w_out", j)]
    return jobs, keys


def _layer(i, grp, states, wts, w16, front_casts, back_casts):
    h, h16, bn, rows, shared = grp["h"], grp["h16"], grp["bn"], grp["rows"], grp["shared"]
    m, d_model = h.shape
    length = m // bn
    j = i // 2
    norm_w = wts["ln_w"][i]

    def in_proj(name, src, col0, n, out_dtype):
        if (name, j) not in w16:
            out, w16[(name, j)] = _matmul_cast_norm(h16, wts[src], j, col0, n, norm_w, out_dtype)
            return out
        return _matmul_norm(h16, w16[(name, j)], out_dtype)

    def out_proj(y, name, casts):
        if (name, j) not in w16:
            out, out16, w16[(name, j)] = _matmul_cast_res(y, wts[name], j, h)
        else:
            jobs, keys = casts
            out, out16, *cast = _matmul_res(y, w16[(name, j)], h, jobs)
            w16.update(zip(keys, cast))
        grp["h"], grp["h16"] = out, out16

    if i % 2 == 0:
        ssm, hist_x, hist_bc = states
        d_inner = wts["ssd_w_out"].shape[1]
        n_all = wts["ssd_w_in"].shape[2]
        n_main = n_all - wts["ssd_dt_bias"].shape[1]
        if ("ssd_w_main", j) in w16:
            jobs, keys = front_casts
            proj, dt_raw, *cast = _matmul_norm(h16, w16[("ssd_w_main", j)], bf16, w16[("ssd_w_dt", j)],
                                               jobs)
            w16.update(zip(keys, cast))
        else:
            proj = in_proj("ssd_w_main", "ssd_w_in", 0, n_main, bf16)
            dt_raw = in_proj("ssd_w_dt", "ssd_w_in", n_main, n_all - n_main, f32)
        conv_w, conv_b = wts["ssd_conv_w"][j], wts["ssd_conv_b"][j]
        n_ssd = wts["ssd_w_out"].shape[0]
        y, hx, hbc, *new_ssm = _ssd_core(
            proj.reshape(bn, length, -1), dt_raw.reshape(bn, length, -1), hist_x, hist_bc, ssm,
            conv_w[:, :d_inner], conv_w[:, d_inner:], conv_b[:d_inner], conv_b[d_inner:],
            wts["ssd_dt_bias"][j], wts["ssd_a_log"][j], wts["ssd_d_wide"][j],
            wts["ssd_norm_w"][j], rows, shared, 0 if shared else j * (bn - 1),
            (grp["ssm_all"], n_ssd, j))
        grp["ssm_all"] = new_ssm[-1]
        yield new_ssm[0], hx, hbc
        out_proj(y.reshape(m, d_inner), "ssd_w_out", back_casts)
        return
    (hist,) = states
    if shared:
        jobs, keys = front_casts
        y, hist, *cast = _gated_conv_proj(h16, w16[("sc_w_in", j)], wts["sc_conv_w"][j], hist, bn, jobs)
        w16.update(zip(keys, cast))
    else:
        proj = in_proj("sc_w_in", "sc_w_in", 0, wts["sc_w_in"].shape[2], bf16)
        y, hist = _sc_core(proj.reshape(bn, length, -1), hist, wts["sc_conv_w"][j], rows, shared)
    yield (hist,)
    out_proj(y.reshape(m, -1), "sc_w_out", back_casts)


def kernel(x_prompt, x_sample, state_ssm, state_ssd_conv, state_sc_conv, meta_tokens, ln_w, ssd_w_in,
           ssd_conv_w, ssd_conv_b, ssd_dt_bias, ssd_a_log, ssd_d_skip, ssd_norm_w, ssd_w_out,
           sc_w_in, sc_conv_w, sc_w_out, final_norm_w):
    batch, seq, d_model = x_prompt.shape
    dec_batch, dec_seq, _ = x_sample.shape
    n_ssd, _, n_heads, head_dim, d_state = state_ssm.shape
    n_sc = state_sc_conv.shape[0]
    d_inner = n_heads * head_dim
    assert head_dim == HEAD_DIM and d_state == D_STATE
    assert meta_tokens.shape[0] == dec_seq and dec_seq % (2 * SUBLANES) == 0 and dec_seq <= CHUNK
    assert seq % CHUNK == 0

    wts = {
        "ln_w": ln_w, "final_norm_w": final_norm_w,
        "ssd_w_in": ssd_w_in, "ssd_w_out": ssd_w_out, "sc_w_in": sc_w_in, "sc_w_out": sc_w_out,
        "ssd_conv_w": ssd_conv_w, "ssd_conv_b": ssd_conv_b, "ssd_dt_bias": ssd_dt_bias,
        "ssd_a_log": ssd_a_log, "ssd_norm_w": ssd_norm_w, "sc_conv_w": sc_conv_w,
        "ssd_d_wide": jnp.repeat(ssd_d_skip, head_dim, axis=1),
    }

    n_small = 1 + dec_batch
    h_small = jnp.concatenate([meta_tokens[None].astype(x_sample.dtype), x_sample], axis=0)
    zero_lead = lambda a: jnp.concatenate([jnp.zeros((1,) + a.shape[1:], a.dtype), a], axis=0)
    ssm_in = state_ssm.reshape(n_ssd * dec_batch, d_inner, d_state)
    hx_in, hbc_in, sc_in = [], [], []
    for j in range(n_ssd):
        hist = _pad_history(zero_lead(state_ssd_conv[j]))
        hx_in.append(hist[:, :, :d_inner])
        hbc_in.append(hist[:, :, d_inner:])
    for j in range(n_sc):
        sc_in.append(_pad_history(zero_lead(state_sc_conv[j])))

    def group(h, bn, rows, shared):
        h = h.reshape(-1, d_model)
        return dict(h=h, h16=h.astype(bf16), bn=bn, rows=rows, shared=shared, ssm_all=None)

    small = group(h_small, n_small, dec_seq, False)
    prompt = group(x_prompt, batch, CHUNK, True)
    w16 = {}
    hx_s, hbc_s, sc_s, hx_p, hbc_p, sc_p = ([] for _ in range(6))
    depth = ln_w.shape[0]
    for i in range(depth):
        j = i // 2
        ssd = i % 2 == 0
        none = ((), ())
        run_s = _layer(i, small, (ssm_in, hx_in[j], hbc_in[j]) if ssd else (sc_in[j],),
                       wts, w16, none, none)
        new_s = next(run_s)
        nxt = _cast_jobs(i + 1, wts) if i + 1 < depth else none
        if ssd:
            own = ("ssd_w_out", j)
            front = none if own in w16 else ([(ssd_w_out, j, None, [(0, d_model)])], [own])
            back = nxt
        else:
            front, back = nxt, none
        run_p = _layer(i, prompt, tuple(a[0:1] for a in new_s), wts, w16, front, back)
        new_p = next(run_p)
        for run in (run_s, run_p):
            next(run, None)
        for dst, val in zip((hx_s, hbc_s) if ssd else (sc_s,), new_s[1:] if ssd else new_s):
            dst.append(val)
        for dst, val in zip((hx_p, hbc_p) if ssd else (sc_p,), new_p[1:] if ssd else new_p):
            dst.append(val)
    y_small = _rmsnorm(small["h"], final_norm_w, f32)
    y_p = _rmsnorm(prompt["h"], final_norm_w, f32)

    def ssm_out(grp):
        return grp["ssm_all"].reshape(n_ssd, -1, n_heads, head_dim, d_state)

    def ssd_hist_out(px, pbc, lo, k):
        return jnp.stack([jnp.concatenate([a[lo:, HALO - k:], b[lo:, HALO - k:]], axis=-1)
                          for a, b in zip(px, pbc)])

    def sc_hist_out(parts, lo, k):
        return jnp.stack([a[lo:, HALO - k:] for a in parts])

    k_ssd = state_ssd_conv.shape[2]
    k_sc = state_sc_conv.shape[2]
    y_prompt = y_p.reshape(batch, seq, d_model)
    y_sample = y_small.reshape(n_small, dec_seq, d_model)[1:]
    return (y_prompt, y_sample,
            ssm_out(prompt), ssd_hist_out(hx_p, hbc_p, 0, k_ssd), sc_hist_out(sc_p, 0, k_sc),
            ssm_out(small), ssd_hist_out(hx_s, hbc_s, 1, k_ssd), sc_hist_out(sc_s, 1, k_sc))
```

```python
import functools

import jax
import jax.numpy as jnp
from jax import lax
from jax.experimental import pallas as pl
from jax.experimental.pallas import tpu as pltpu

EPS = 1e-5
HEAD_DIM = 64
D_STATE = 128
N_GROUPS = 8
LANES = 128
SUBLANES = 8
CHUNK = 128
HALO = SUBLANES
VMEM_LIMIT = 56 * 1024 * 1024
LOG2_E = 1.4426950408889634

f32 = jnp.float32
bf16 = jnp.bfloat16


def _sigmoid(x):
    return 1.0 / (1.0 + jnp.exp2(x * (-LOG2_E)))


def _causal_conv(cur, prev, w):
    taps = w.shape[0]
    row = lax.broadcasted_iota(jnp.int32, prev.shape, 0)
    acc = None
    for k in range(taps):
        d = taps - 1 - k
        part = cur
        if d > 0:
            rolled = pltpu.roll(cur, d, 0)
            head = jnp.where(row < d, pltpu.roll(prev, d, 0), rolled[0:HALO])
            part = jnp.concatenate([head, rolled[HALO:]], axis=0)
        term = part * w[k:k + 1, :]
        acc = term if acc is None else acc + term
    return acc


def _params(*sem):
    return pltpu.CompilerParams(dimension_semantics=sem, vmem_limit_bytes=VMEM_LIMIT)


def _rmsnorm_kernel(x_ref, w_ref, o_ref):
    x = x_ref[...]
    ms = jnp.mean(x * x, axis=-1, keepdims=True)
    o_ref[...] = (x * lax.rsqrt(ms + EPS) * w_ref[...]).astype(o_ref.dtype)


def _rmsnorm(x, w, out_dtype):
    m, d = x.shape
    tm = 256 if m % 256 == 0 else m
    return pl.pallas_call(
        _rmsnorm_kernel,
        grid=(m // tm,),
        in_specs=[pl.BlockSpec((tm, d), lambda i: (i, 0)),
                  pl.BlockSpec((1, d), lambda i: (0, 0))],
        out_specs=pl.BlockSpec((tm, d), lambda i: (i, 0)),
        out_shape=jax.ShapeDtypeStruct((m, d), out_dtype),
        compiler_params=_params("arbitrary"),
        name="rmsnorm",
    )(x, w.reshape(1, d))


def _row_scale(h16):
    hf = h16.astype(f32)
    return lax.rsqrt(jnp.mean(hf * hf, axis=-1, keepdims=True) + EPS)


def _mm_norm_kernel(h_ref, w_ref, *refs, extra):
    if extra:
        we_ref, o_ref, oe_ref, r_ref = refs
    else:
        o_ref, r_ref = refs

    @pl.when(pl.program_id(1) == 0)
    def _():
        r = _row_scale(h_ref[...])
        r_ref[...] = r
        if extra:
            oe_ref[...] = jnp.dot(h_ref[...], we_ref[...], preferred_element_type=f32) * r

    acc = jnp.dot(h_ref[...], w_ref[...], preferred_element_type=f32)
    o_ref[...] = (acc * r_ref[...]).astype(o_ref.dtype)


def _cast_specs(casts, n_col_steps, n_steps):
    in_specs, args, out_specs, out_shape, layout = [], [], [], [], []
    step = lambda i, j: i * n_col_steps + j
    for w32, layer, norm_w, splits in casts:
        rows, cols = w32.shape[1:]
        rb = rows // n_steps
        assert rb * n_steps == rows and rb % (2 * SUBLANES) == 0
        in_specs.append(pl.BlockSpec((None, rb, cols), lambda i, j, layer=layer: (layer, step(i, j), 0)))
        args.append(w32)
        if norm_w is not None:
            in_specs.append(pl.BlockSpec((rb, 1), lambda i, j: (step(i, j), 0)))
            args.append(norm_w.reshape(rows, 1))
        for lo, hi in splits:
            out_specs.append(pl.BlockSpec((rb, hi - lo), lambda i, j: (step(i, j), 0)))
            out_shape.append(jax.ShapeDtypeStruct((rows, hi - lo), bf16))
        layout.append((norm_w is not None, tuple(splits)))
    return in_specs, args, out_specs, out_shape, tuple(layout)


def _n_cast_inputs(layout):
    return sum(2 if has_norm else 1 for has_norm, _ in layout)


def _run_casts(layout, in_refs, out_refs):
    src = dst = 0
    for has_norm, splits in layout:
        block = in_refs[src][...]
        if has_norm:
            block = block * in_refs[src + 1][...]
        src += 2 if has_norm else 1
        for lo, hi in splits:
            out_refs[dst][...] = block[:, lo:hi].astype(bf16)
            dst += 1


def _mm_res_kernel(x_ref, w_ref, r_ref, *refs, side):
    n_in = _n_cast_inputs(side)
    o_ref, o16_ref = refs[n_in], refs[n_in + 1]
    out = r_ref[...] + jnp.dot(x_ref[...], w_ref[...], preferred_element_type=f32)
    o_ref[...] = out
    o16_ref[...] = out.astype(bf16)
    _run_casts(side, refs[:n_in], refs[n_in + 2:])


def _pick(n, candidates):
    for c in candidates:
        if n % c == 0:
            return c
    return n


def _mm_tiles(m, k, n, with_res):
    tm = _pick(m, (1024, 512, 256))
    budget = 48 * 1024 * 1024
    out_bytes = 4 + 4 + 2 if with_res else 4
    for tn in (1024, 512, 256, 128):
        if n % tn:
            continue
        need = 2 * (tm * k * 2 + k * tn * 2 + tm * tn * out_bytes)
        if need <= budget:
            return tm, tn
    return tm, _pick(n, (128,))


def _matmul_norm(h16, w, out_dtype, w_extra=None):
    m, k = h16.shape
    n = w.shape[1]
    tm, tn = _mm_tiles(m, k, n, False)
    in_specs = [pl.BlockSpec((tm, k), lambda i, j: (i, 0)),
                pl.BlockSpec((k, tn), lambda i, j: (0, j))]
    out_specs = [pl.BlockSpec((tm, tn), lambda i, j: (i, j))]
    out_shape = [jax.ShapeDtypeStruct((m, n), out_dtype)]
    args = [h16, w]
    if w_extra is not None:
        ne = w_extra.shape[1]
        in_specs.append(pl.BlockSpec((k, ne), lambda i, j: (0, 0)))
        out_specs.append(pl.BlockSpec((tm, ne), lambda i, j: (i, 0)))
        out_shape.append(jax.ShapeDtypeStruct((m, ne), f32))
        args.append(w_extra)
    res = pl.pallas_call(
        functools.partial(_mm_norm_kernel, extra=w_extra is not None),
        grid=(m // tm, n // tn),
        in_specs=in_specs,
        out_specs=out_specs,
        out_shape=out_shape,
        scratch_shapes=[pltpu.VMEM((tm, 1), f32)],
        compiler_params=_params("arbitrary", "arbitrary"),
        name="matmul_norm",
    )(*args)
    return res if w_extra is not None else res[0]


def _matmul_res(x, w, res, casts=()):
    m, k = x.shape
    n = w.shape[1]
    tm, tn = _mm_tiles(m, k, n, True)
    nj = n // tn
    tile = pl.BlockSpec((tm, tn), lambda i, j: (i, j))
    c_in, c_args, c_out, c_shape, side = _cast_specs(casts, nj, (m // tm) * nj)
    return pl.pallas_call(
        functools.partial(_mm_res_kernel, side=side),
        grid=(m // tm, nj),
        in_specs=[pl.BlockSpec((tm, k), lambda i, j: (i, 0)),
                  pl.BlockSpec((k, tn), lambda i, j: (0, j)),
                  tile] + c_in,
        out_specs=[tile, tile] + c_out,
        out_shape=[jax.ShapeDtypeStruct((m, n), f32), jax.ShapeDtypeStruct((m, n), bf16)] + c_shape,
        compiler_params=_params("arbitrary", "arbitrary"),
        name="matmul_res",
    )(x, w, res, *c_args)


def _mm_cast_norm_kernel(h_ref, w_ref, nw_ref, o_ref, wb_ref):
    wb = (w_ref[...] * nw_ref[...]).astype(bf16)
    wb_ref[...] = wb
    acc = jnp.dot(h_ref[...], wb, preferred_element_type=f32)
    o_ref[...] = (acc * _row_scale(h_ref[...])).astype(o_ref.dtype)


def _mm_cast_res_kernel(x_ref, w_ref, r_ref, o_ref, o16_ref, wb_ref):
    wb = w_ref[...].astype(bf16)
    wb_ref[...] = wb
    out = r_ref[...] + jnp.dot(x_ref[...], wb, preferred_element_type=f32)
    o_ref[...] = out
    o16_ref[...] = out.astype(bf16)


def _cast_tile(k, n, col0):
    tile_bytes = 8 * 1024 * 1024
    return next(t for t in (512, 256, 128) if n % t == 0 and col0 % t == 0 and k * t * 4 <= tile_bytes)


def _matmul_cast_norm(h16, w32, layer, col0, n, norm_w, out_dtype):
    m, k = h16.shape
    tn = _cast_tile(k, n, col0)
    j0 = col0 // tn
    return pl.pallas_call(
        _mm_cast_norm_kernel,
        grid=(n // tn,),
        in_specs=[pl.BlockSpec((m, k), lambda j: (0, 0)),
                  pl.BlockSpec((None, k, tn), lambda j: (layer, 0, j0 + j)),
                  pl.BlockSpec((k, 1), lambda j: (0, 0))],
        out_specs=[pl.BlockSpec((m, tn), lambda j: (0, j)),
                   pl.BlockSpec((k, tn), lambda j: (0, j))],
        out_shape=[jax.ShapeDtypeStruct((m, n), out_dtype),
                   jax.ShapeDtypeStruct((k, n), bf16)],
        compiler_params=_params("arbitrary"),
        name="matmul_cast_norm",
    )(h16, w32, norm_w.reshape(k, 1))


def _matmul_cast_res(x, w32, layer, res):
    m, k = x.shape
    n = w32.shape[2]
    tn = _cast_tile(k, n, 0)
    tile = pl.BlockSpec((m, tn), lambda j: (0, j))
    return pl.pallas_call(
        _mm_cast_res_kernel,
        grid=(n // tn,),
        in_specs=[pl.BlockSpec((m, k), lambda j: (0, 0)),
                  pl.BlockSpec((None, k, tn), lambda j: (layer, 0, j)),
                  tile],
        out_specs=[tile, tile, pl.BlockSpec((k, tn), lambda j: (0, j))],
        out_shape=[jax.ShapeDtypeStruct((m, n), f32), jax.ShapeDtypeStruct((m, n), bf16),
                   jax.ShapeDtypeStruct((k, n), bf16)],
        compiler_params=_params("arbitrary"),
        name="matmul_cast_res",
    )(x, w32, res)


def _gated_conv_proj_kernel(x_ref, wg_ref, wb_ref, wc_ref, wv_ref, cw_ref, hist_ref, *refs,
                            tiles_per_stream, side):
    n_in = _n_cast_inputs(side)
    o_ref, ho_ref = refs[n_in], refs[n_in + 1]
    halo, r_ref = refs[-2], refs[-1]
    _run_casts(side, refs[:n_in], refs[n_in + 2:-2])
    i = pl.program_id(0)
    j = pl.program_id(1)

    @pl.when(i % tiles_per_stream == 0)
    def _():
        halo[j] = hist_ref[0]

    @pl.when(j == 0)
    def _():
        r_ref[...] = _row_scale(x_ref[...])

    x = x_ref[...]
    r = r_ref[...]
    cur = (jnp.dot(x, wc_ref[...], preferred_element_type=f32)
           * jnp.dot(x, wv_ref[...], preferred_element_type=f32)) * (r * r)
    cv = _causal_conv(cur, halo[j], cw_ref[...])
    gate = jnp.dot(x, wg_ref[...], preferred_element_type=f32) * r
    o_ref[...] = ((gate * _sigmoid(gate)) * (jnp.dot(x, wb_ref[...], preferred_element_type=f32) * r)
                  * cv).astype(o_ref.dtype)
    tail = cur[cur.shape[0] - HALO:]
    halo[j] = tail
    ho_ref[0] = tail


def _proj_rows(stream_len):
    return _pick(stream_len, (1024, 512, 256, 128))


def _gated_conv_proj(x, w, conv_w, hist, bn, casts=()):
    m, k = x.shape
    width = w.shape[1] // 4
    tm, tc = _proj_rows(m // bn), 256
    nj = width // tc
    tiles_per_stream = m // bn // tm
    taps = conv_w.shape[0]
    c_in, c_args, c_out, c_shape, side = _cast_specs(casts, nj, (m // tm) * nj)
    kern = functools.partial(_gated_conv_proj_kernel, tiles_per_stream=tiles_per_stream, side=side)
    wspec = lambda part: pl.BlockSpec((k, tc), lambda i, j: (0, part * nj + j))
    out, tails, *cast = pl.pallas_call(
        kern,
        grid=(m // tm, nj),
        in_specs=[pl.BlockSpec((tm, k), lambda i, j: (i, 0)),
                  wspec(0), wspec(1), wspec(2), wspec(3),
                  pl.BlockSpec((taps, tc), lambda i, j: (0, j)),
                  pl.BlockSpec((1, HALO, tc), lambda i, j: (0, 0, j))] + c_in,
        out_specs=[pl.BlockSpec((tm, tc), lambda i, j: (i, j)),
                   pl.BlockSpec((1, HALO, tc), lambda i, j: (i, 0, j))] + c_out,
        out_shape=[jax.ShapeDtypeStruct((m, width), bf16),
                   jax.ShapeDtypeStruct((m // tm, HALO, width), f32)] + c_shape,
        scratch_shapes=[pltpu.VMEM((nj, HALO, tc), f32), pltpu.VMEM((tm, 1), f32)],
        compiler_params=_params("arbitrary", "arbitrary"),
        name="gated_conv_proj",
    )(x, w, w, w, w, conv_w, hist, *c_args)
    return (out, tails[tiles_per_stream - 1::tiles_per_stream], *cast)


def _shift_matrix(taps, q):
    t = jnp.arange(q)[:, None]
    s = jnp.arange(q)[None, :]
    return jnp.concatenate([(s == t - (taps - 1 - k)) for k in range(taps)], axis=1).astype(bf16)


def _conv_silu(x16, hist, shift, w_ref, b_ref, lo, hi, taps, rows):
    w = w_ref[:, lo:hi]
    w16 = w.astype(bf16)
    scaled = jnp.concatenate([x16 * w16[k:k + 1, :] for k in range(taps)], axis=0)
    acc = jnp.dot(shift, scaled, preferred_element_type=f32)
    row = lax.broadcasted_iota(jnp.int32, hist.shape, 0)
    head = acc[0:HALO]
    for k in range(taps - 1):
        d = taps - 1 - k
        head = head + jnp.where(row < d, pltpu.roll(hist, d, 0), 0.0) * w[k:k + 1, :]
    acc = jnp.concatenate([head, acc[HALO:]], axis=0) + b_ref[:, lo:hi]
    tail = x16[rows - 2 * HALO:rows].astype(f32)[HALO:]
    return acc * _sigmoid(acc), tail


def _ssd_kernel(z_ref, x_ref, bc_ref, dt_ref, cxi_ref, cbci_ref, si_ref, shift_ref,
                cwx_ref, cwbc_ref, cbx_ref, cbbc_ref, dtb_ref, alog_ref, dsk_ref, nw_ref,
                *refs, rows, n_chunks, taps, first_stream_empty):
    y_ref, cxo_ref, cbco_ref = refs[1:4]
    so_refs = refs[4:-3]
    hx, hbc, st = refs[-3:]
    q = CHUNK
    d_inner = x_ref.shape[2]
    gw = d_inner // N_GROUPS
    heads_per_group = gw // HEAD_DIM
    c = pl.program_id(1)

    @pl.when(c == 0)
    def _():
        hx[...] = cxi_ref[0]
        hbc[...] = cbci_ref[0]
        for blk in range(d_inner // LANES):
            st[:, blk * LANES:(blk + 1) * LANES] = si_ref[0, blk * LANES:(blk + 1) * LANES, :].T

    if first_stream_empty:
        @pl.when((c == 0) & (pl.program_id(0) == 0))
        def _():
            st[...] = jnp.zeros_like(st)

    def pad_rows(v):
        if rows == q:
            return v
        return jnp.concatenate([v, jnp.zeros((q - rows, v.shape[1]), v.dtype)], axis=0)

    row_i = lax.broadcasted_iota(jnp.int32, (q, q), 0)
    col_i = lax.broadcasted_iota(jnp.int32, (q, q), 1)
    causal = row_i >= col_i
    v = pad_rows(dt_ref[0]) + dtb_ref[...]
    dt = jnp.maximum(v, 0.0) + jnp.log1p(jnp.exp(-jnp.abs(v)))
    if rows < q:
        dt = jnp.where(lax.broadcasted_iota(jnp.int32, dt.shape, 0) < rows, dt, 0.0)
    a = dt * (-jnp.exp(alog_ref[...]))
    acs = jnp.dot(causal.astype(f32), a, precision=lax.Precision.HIGHEST,
                  preferred_element_type=f32) * LOG2_E
    acs_t = acs.T
    src_t = acs_t - jnp.log2(dt.T)
    w_t = jnp.exp2(acs_t[:, q - 1:q] - src_t)

    n_state_cols = D_STATE * N_GROUPS
    bc16 = pad_rows(bc_ref[0])
    shift = shift_ref[...]
    b_all, b_tail = _conv_silu(bc16[:, 0:n_state_cols], hbc[:, 0:n_state_cols], shift,
                               cwbc_ref, cbbc_ref, 0, n_state_cols, taps, rows)
    c_all, c_tail = _conv_silu(bc16[:, n_state_cols:], hbc[:, n_state_cols:], shift,
                               cwbc_ref, cbbc_ref, n_state_cols, 2 * n_state_cols, taps, rows)
    hbc[:, 0:n_state_cols] = b_tail
    hbc[:, n_state_cols:] = c_tail
    lane = lax.broadcasted_iota(jnp.int32, (q, LANES), 1)
    first_head = lane < HEAD_DIM
    neg_inf = jnp.float32(-jnp.inf)

    for g in range(N_GROUPS):
        glo = g * gw
        b_g = b_all[:, g * D_STATE:(g + 1) * D_STATE]
        c_g = c_all[:, g * D_STATE:(g + 1) * D_STATE].astype(bf16)
        cb = lax.dot_general(c_g, b_g.astype(bf16), (((1,), (1,)), ((), ())),
                             preferred_element_type=f32)
        b_t = b_g.T
        s_g = st[:, glo:glo + gw]
        y_off = jnp.dot(c_g, s_g.astype(bf16), preferred_element_type=f32)
        x_g, x_tail = _conv_silu(pad_rows(x_ref[0, :, glo:glo + gw]), hx[:, glo:glo + gw], shift,
                                 cwx_ref, cbx_ref, glo, glo + gw, taps, rows)
        hx[:, glo:glo + gw] = x_tail

        y_parts = []
        for p in range(heads_per_group // 2):
            plo = p * LANES
            x_p = x_g[:, plo:plo + LANES]
            m_parts, bw_parts, e_parts = [], [], []
            for hh in range(2):
                h = g * heads_per_group + 2 * p + hh
                col_b = jnp.broadcast_to(acs[:, h:h + 1], (q, q))
                seg = col_b - src_t[h:h + 1, :]
                m_parts.append(cb * jnp.exp2(jnp.where(causal, seg, neg_inf)))
                bw_parts.append(b_t * w_t[h:h + 1, :])
                e_parts.append(jnp.exp2(col_b))
            lhs = jnp.concatenate([jnp.concatenate(m_parts, axis=1),
                                   jnp.concatenate(bw_parts, axis=1)], axis=0).astype(bf16)
            x_bd = jnp.concatenate([jnp.where(first_head, x_p, 0.0),
                                    jnp.where(first_head, 0.0, x_p)], axis=0).astype(bf16)
            res = jnp.dot(lhs, x_bd, preferred_element_type=f32)
            e_sel = jnp.where(first_head, e_parts[0], e_parts[1])
            y_parts.append(res[0:q] + y_off[:, plo:plo + LANES] * e_sel
                           + x_p * dsk_ref[:, glo + plo:glo + plo + LANES])
            st[:, glo + plo:glo + plo + LANES] = (s_g[:, plo:plo + LANES] * e_sel[q - 1:q, :]
                                                  + res[q:q + D_STATE])
        y_g = jnp.concatenate(y_parts, axis=1)
        z_g = pad_rows(z_ref[0, :, glo:glo + gw]).astype(f32)
        y_g = y_g * (z_g * _sigmoid(z_g))
        ms = jnp.mean(y_g * y_g, axis=-1, keepdims=True)
        y_n = y_g * lax.rsqrt(ms + EPS) * nw_ref[:, glo:glo + gw]
        y_ref[0, :, glo:glo + gw] = y_n[0:rows].astype(y_ref.dtype)

    last = c == n_chunks - 1

    def write_state(so_ref):
        for blk in range(d_inner // LANES):
            so_ref[0, blk * LANES:(blk + 1) * LANES, :] = st[:, blk * LANES:(blk + 1) * LANES].T

    @pl.when(last)
    def _():
        cxo_ref[0] = hx[...]
        cbco_ref[0] = hbc[...]

    if first_stream_empty:
        @pl.when(last & (pl.program_id(0) == 0))
        def _():
            write_state(so_refs[0])

        @pl.when(last & (pl.program_id(0) > 0))
        def _():
            write_state(so_refs[1])
    else:
        @pl.when(last)
        def _():
            write_state(so_refs[0])


def _ssd_core(proj, dt_raw, conv_x, conv_bc, state, cw_x, cw_bc, cb_x, cb_bc, dt_bias, a_log,
              d_skip_wide, norm_w, rows, shared_state, state_base, state_out):
    bn, length, _ = proj.shape
    d_inner = cw_x.shape[1]
    n_bc = cw_bc.shape[1]
    n_heads = dt_bias.shape[0]
    taps = cw_x.shape[0]
    n_chunks = length // rows
    first_stream_empty = not shared_state
    earlier, n_layers, layer = state_out
    sidx = (lambda b, c: (0, 0, 0)) if shared_state else (lambda b, c: (b, 0, 0))
    rest = lambda b: jnp.maximum(b - 1, 0)
    ssm_idx = sidx if shared_state else (lambda b, c: (state_base + rest(b), 0, 0))
    const = lambda b, c: (0, 0)
    vec = lambda n: pl.BlockSpec((1, n), const)
    slab = (None, 1, d_inner, D_STATE)
    if first_stream_empty:
        state_specs = [pl.BlockSpec((1, d_inner, D_STATE), lambda b, c: (0, 0, 0)),
                       pl.BlockSpec(slab, lambda b, c: (layer, rest(b), 0, 0))]
        state_shapes = [jax.ShapeDtypeStruct((1, d_inner, D_STATE), f32),
                        jax.ShapeDtypeStruct((n_layers, bn - 1, d_inner, D_STATE), f32)]
    else:
        state_specs = [pl.BlockSpec(slab, lambda b, c: (layer, b, 0, 0))]
        state_shapes = [jax.ShapeDtypeStruct((n_layers, bn, d_inner, D_STATE), f32)]
    kern = functools.partial(_ssd_kernel, rows=rows, n_chunks=n_chunks, taps=taps,
                             first_stream_empty=first_stream_empty)
    n_inputs = 16
    extra_in, extra_args = [pl.BlockSpec(memory_space=pl.ANY)], [earlier]
    return pl.pallas_call(
        kern,
        grid=(bn, n_chunks),
        input_output_aliases={n_inputs: 3 + len(state_specs) - 1},
        in_specs=[
            pl.BlockSpec((1, rows, d_inner), lambda b, c: (b, c, 0)),
            pl.BlockSpec((1, rows, d_inner), lambda b, c: (b, c, 1)),
            pl.BlockSpec((1, rows, n_bc), lambda b, c: (b, c, 2 * d_inner // n_bc)),
            pl.BlockSpec((1, rows, n_heads), lambda b, c: (b, c, 0)),
            pl.BlockSpec((1, HALO, d_inner), sidx),
            pl.BlockSpec((1, HALO, n_bc), sidx),
            pl.BlockSpec((1, d_inner, D_STATE), ssm_idx),
            pl.BlockSpec((CHUNK, taps * CHUNK), const),
            pl.BlockSpec((taps, d_inner), const),
            pl.BlockSpec((taps, n_bc), const),
            vec(d_inner), vec(n_bc), vec(n_heads), vec(n_heads), vec(d_inner), vec(d_inner),
        ] + extra_in,
        out_specs=[
            pl.BlockSpec((1, rows, d_inner), lambda b, c: (b, c, 0)),
            pl.BlockSpec((1, HALO, d_inner), lambda b, c: (b, 0, 0)),
            pl.BlockSpec((1, HALO, n_bc), lambda b, c: (b, 0, 0)),
        ] + state_specs,
        out_shape=[
            jax.ShapeDtypeStruct((bn, length, d_inner), bf16),
            jax.ShapeDtypeStruct((bn, HALO, d_inner), f32),
            jax.ShapeDtypeStruct((bn, HALO, n_bc), f32),
        ] + state_shapes,
        scratch_shapes=[
            pltpu.VMEM((HALO, d_inner), f32),
            pltpu.VMEM((HALO, n_bc), f32),
            pltpu.VMEM((D_STATE, d_inner), f32),
        ],
        compiler_params=_params("arbitrary", "arbitrary"),
        name="ssd_core",
    )(proj, proj, proj, dt_raw, conv_x, conv_bc, state, _shift_matrix(taps, CHUNK), cw_x, cw_bc,
      cb_x.reshape(1, -1), cb_bc.reshape(1, -1), dt_bias.reshape(1, -1), a_log.reshape(1, -1),
      d_skip_wide.reshape(1, -1), norm_w.reshape(1, -1), *extra_args)


def _sc_kernel(g_ref, b_ref, c_ref, v_ref, ci_ref, w_ref, y_ref, co_ref, hist, *, rows, n_steps, taps):
    t = pl.program_id(2)

    @pl.when(t == 0)
    def _():
        hist[...] = ci_ref[0]

    cur = c_ref[0].astype(f32) * v_ref[0].astype(f32)
    cv = _causal_conv(cur, hist[...], w_ref[...])
    gate = g_ref[0].astype(f32)
    y_ref[0] = ((gate * _sigmoid(gate)) * b_ref[0].astype(f32) * cv).astype(y_ref.dtype)
    hist[...] = cur[rows - HALO:rows]

    @pl.when(t == n_steps - 1)
    def _():
        co_ref[0] = hist[...]


def _sc_core(proj, conv_in, conv_w, rows, shared_state):
    bn, length, four_w = proj.shape
    width = four_w // 4
    taps = conv_w.shape[0]
    tc = _pick(width, (1024, 512, 256, 128))
    nj = width // tc
    n_steps = length // rows
    sidx = (lambda b, j, t: (0, 0, j)) if shared_state else (lambda b, j, t: (b, 0, j))
    kern = functools.partial(_sc_kernel, rows=rows, n_steps=n_steps, taps=taps)
    col = lambda part: (lambda b, j, t: (b, t, part * nj + j))
    return pl.pallas_call(
        kern,
        grid=(bn, nj, n_steps),
        in_specs=[pl.BlockSpec((1, rows, tc), col(0)),
                  pl.BlockSpec((1, rows, tc), col(1)),
                  pl.BlockSpec((1, rows, tc), col(2)),
                  pl.BlockSpec((1, rows, tc), col(3)),
                  pl.BlockSpec((1, HALO, tc), sidx),
                  pl.BlockSpec((taps, tc), lambda b, j, t: (0, j))],
        out_specs=[pl.BlockSpec((1, rows, tc), lambda b, j, t: (b, t, j)),
                   pl.BlockSpec((1, HALO, tc), lambda b, j, t: (b, 0, j))],
        out_shape=[jax.ShapeDtypeStruct((bn, length, width), bf16),
                   jax.ShapeDtypeStruct((bn, HALO, width), f32)],
        scratch_shapes=[pltpu.VMEM((HALO, tc), f32)],
        compiler_params=_params("arbitrary", "arbitrary", "arbitrary"),
        name="sc_core",
    )(proj, proj, proj, proj, conv_in, conv_w)


def _pad_history(buf):
    return jnp.pad(buf, ((0, 0), (HALO - buf.shape[1], 0), (0, 0)))


def _cast_jobs(i, wts):
    j = i // 2
    norm_w = wts["ln_w"][i]
    if i % 2 == 0:
        n_all = wts["ssd_w_in"].shape[2]
        n_main = n_all - wts["ssd_dt_bias"].shape[1]
        jobs = [(wts["ssd_w_in"], j, norm_w, [(0, n_main), (n_main, n_all)]),
                (wts["ssd_w_out"], j, None, [(0, wts["ssd_w_out"].shape[2])])]
        keys = [("ssd_w_main", j), ("ssd_w_dt", j), ("ssd_w_out", j)]
    else:
        jobs = [(wts["sc_w_in"], j, norm_w, [(0, wts["sc_w_in"].shape[2])]),
                (wts["sc_w_out"], j, None, [(0, wts["sc_w_out"].shape[2])])]
        keys = [("sc_w_in", j), ("sc_w_out", j)]
    return jobs, keys


def _layer(i, grp, states, wts, w16, next_casts):
    h, h16, bn, rows, shared = grp["h"], grp["h16"], grp["bn"], grp["rows"], grp["shared"]
    m, d_model = h.shape
    length = m // bn
    j = i // 2
    norm_w = wts["ln_w"][i]

    def in_proj(name, src, col0, n, out_dtype):
        if (name, j) not in w16:
            out, w16[(name, j)] = _matmul_cast_norm(h16, wts[src], j, col0, n, norm_w, out_dtype)
            return out
        return _matmul_norm(h16, w16[(name, j)], out_dtype)

    def out_proj(y, name, casts):
        if (name, j) not in w16:
            out, out16, w16[(name, j)] = _matmul_cast_res(y, wts[name], j, h)
        else:
            jobs, keys = casts
            out, out16, *cast = _matmul_res(y, w16[(name, j)], h, jobs)
            w16.update(zip(keys, cast))
        grp["h"], grp["h16"] = out, out16

    if i % 2 == 0:
        ssm, hist_x, hist_bc = states
        d_inner = wts["ssd_w_out"].shape[1]
        n_all = wts["ssd_w_in"].shape[2]
        n_main = n_all - wts["ssd_dt_bias"].shape[1]
        if ("ssd_w_main", j) in w16:
            proj, dt_raw = _matmul_norm(h16, w16[("ssd_w_main", j)], bf16, w16[("ssd_w_dt", j)])
        else:
            proj = in_proj("ssd_w_main", "ssd_w_in", 0, n_main, bf16)
            dt_raw = in_proj("ssd_w_dt", "ssd_w_in", n_main, n_all - n_main, f32)
        conv_w, conv_b = wts["ssd_conv_w"][j], wts["ssd_conv_b"][j]
        n_ssd = wts["ssd_w_out"].shape[0]
        y, hx, hbc, *new_ssm = _ssd_core(
            proj.reshape(bn, length, -1), dt_raw.reshape(bn, length, -1), hist_x, hist_bc, ssm,
            conv_w[:, :d_inner], conv_w[:, d_inner:], conv_b[:d_inner], conv_b[d_inner:],
            wts["ssd_dt_bias"][j], wts["ssd_a_log"][j], wts["ssd_d_wide"][j],
            wts["ssd_norm_w"][j], rows, shared, 0 if shared else j * (bn - 1),
            (grp["ssm_all"], n_ssd, j))
        grp["ssm_all"] = new_ssm[-1]
        out_proj(y.reshape(m, d_inner), "ssd_w_out", next_casts)
        return new_ssm[0], hx, hbc
    (hist,) = states
    if shared:
        jobs, keys = next_casts
        y, hist, *cast = _gated_conv_proj(h16, w16[("sc_w_in", j)], wts["sc_conv_w"][j], hist, bn, jobs)
        w16.update(zip(keys, cast))
    else:
        proj = in_proj("sc_w_in", "sc_w_in", 0, wts["sc_w_in"].shape[2], bf16)
        y, hist = _sc_core(proj.reshape(bn, length, -1), hist, wts["sc_conv_w"][j], rows, shared)
    out_proj(y.reshape(m, -1), "sc_w_out", ((), ()))
    return (hist,)


def kernel(x_prompt, x_sample, state_ssm, state_ssd_conv, state_sc_conv, meta_tokens, ln_w, ssd_w_in,
           ssd_conv_w, ssd_conv_b, ssd_dt_bias, ssd_a_log, ssd_d_skip, ssd_norm_w, ssd_w_out,
           sc_w_in, sc_conv_w, sc_w_out, final_norm_w):
    batch, seq, d_model = x_prompt.shape
    dec_batch, dec_seq, _ = x_sample.shape
    n_ssd, _, n_heads, head_dim, d_state = state_ssm.shape
    n_sc = state_sc_conv.shape[0]
    d_inner = n_heads * head_dim
    assert head_dim == HEAD_DIM and d_state == D_STATE
    assert meta_tokens.shape[0] == dec_seq and dec_seq % (2 * SUBLANES) == 0 and dec_seq <= CHUNK
    assert seq % CHUNK == 0

    wts = {
        "ln_w": ln_w, "final_norm_w": final_norm_w,
        "ssd_w_in": ssd_w_in, "ssd_w_out": ssd_w_out, "sc_w_in": sc_w_in, "sc_w_out": sc_w_out,
        "ssd_conv_w": ssd_conv_w, "ssd_conv_b": ssd_conv_b, "ssd_dt_bias": ssd_dt_bias,
        "ssd_a_log": ssd_a_log, "ssd_norm_w": ssd_norm_w, "sc_conv_w": sc_conv_w,
        "ssd_d_wide": jnp.repeat(ssd_d_skip, head_dim, axis=1),
    }

    n_small = 1 + dec_batch
    h_small = jnp.concatenate([meta_tokens[None].astype(x_sample.dtype), x_sample], axis=0)
    zero_lead = lambda a: jnp.concatenate([jnp.zeros((1,) + a.shape[1:], a.dtype), a], axis=0)
    ssm_in = state_ssm.reshape(n_ssd * dec_batch, d_inner, d_state)
    hx_in, hbc_in, sc_in = [], [], []
    for j in range(n_ssd):
        hist = _pad_history(zero_lead(state_ssd_conv[j]))
        hx_in.append(hist[:, :, :d_inner])
        hbc_in.append(hist[:, :, d_inner:])
    for j in range(n_sc):
        sc_in.append(_pad_history(zero_lead(state_sc_conv[j])))

    def group(h, bn, rows, shared, n_out_states):
        h = h.reshape(-1, d_model)
        ssm_all = jnp.zeros((n_ssd, n_out_states, d_inner, d_state), f32)
        return dict(h=h, h16=h.astype(bf16), bn=bn, rows=rows, shared=shared, ssm_all=ssm_all)

    small = group(h_small, n_small, dec_seq, False, dec_batch)
    prompt = group(x_prompt, batch, CHUNK, True, batch)
    w16 = {}
    hx_s, hbc_s, sc_s, hx_p, hbc_p, sc_p = ([] for _ in range(6))
    depth = ln_w.shape[0]
    for i in range(depth):
        j = i // 2
        ssd = i % 2 == 0
        new_s = _layer(i, small, (ssm_in, hx_in[j], hbc_in[j]) if ssd else (sc_in[j],),
                       wts, w16, ((), ()))
        casts = _cast_jobs(i + 1, wts) if i + 1 < depth else ((), ())
        new_p = _layer(i, prompt, tuple(a[0:1] for a in new_s), wts, w16, casts)
        for dst, val in zip((hx_s, hbc_s) if ssd else (sc_s,), new_s[1:] if ssd else new_s):
            dst.append(val)
        for dst, val in zip((hx_p, hbc_p) if ssd else (sc_p,), new_p[1:] if ssd else new_p):
            dst.append(val)
    y_small = _rmsnorm(small["h"], final_norm_w, f32)
    y_p = _rmsnorm(prompt["h"], final_norm_w, f32)

    def ssm_out(grp):
        return grp["ssm_all"].reshape(n_ssd, -1, n_heads, head_dim, d_state)

    def ssd_hist_out(px, pbc, lo, k):
        return jnp.stack([jnp.concatenate([a[lo:, HALO - k:], b[lo:, HALO - k:]], axis=-1)
                          for a, b in zip(px, pbc)])

    def sc_hist_out(parts, lo, k):
        return jnp.stack([a[lo:, HALO - k:] for a in parts])

    k_ssd = state_ssd_conv.shape[2]
    k_sc = state_sc_conv.shape[2]
    y_prompt = y_p.reshape(batch, seq, d_model)
    y_sample = y_small.reshape(n_small, dec_seq, d_model)[1:]
    return (y_prompt, y_sample,
            ssm_out(prompt), ssd_hist_out(hx_p, hbc_p, 0, k_ssd), sc_hist_out(sc_p, 0, k_sc),
            ssm_out(small), ssd_hist_out(hx_s, hbc_s, 1, k_ssd), sc_hist_out(sc_s, 1, k_sc))
```
